```python
import math
import jax, jax.numpy as jnp
from jax import lax
import numpy as np

D_MODEL = 1024
BATCH = 2
SEQ = 8192
DEPTH = 4

N_MIXERS = 3
BLK = 128
D_FF = 2816
PLE_DIM = 256
LN_EPS = 1e-5
ROPE_THETA = 10000.0
DEEPNORM_ALPHA = (2 * DEPTH) ** 0.25
DEEPNORM_BETA = (8 * DEPTH) ** -0.25
SB_HEADS = 16
SB_HEAD_DIM = 64
SWA_Q_HEADS = 16
SWA_KV_HEADS = 2
SWA_HEAD_DIM = 64
SWA_WINDOW = 128
DIL_HEADS = 16
DIL_HEAD_DIM = 64
DIL_GROUPS = ((128, 1), (512, 4), (2048, 16))
N_SB = (DEPTH + 2) // 3
N_SWA = (DEPTH + 1) // 3
N_DIL = DEPTH // 3
NEG_INF = -1e30

kernel_name = 'hybrid_interleaved_sb_swa_dilated_macaron'


def _layernorm(x, g, b):
    xf = x.astype(jnp.float32)
    mu = xf.mean(-1, keepdims=True)
    var = jnp.square(xf - mu).mean(-1, keepdims=True)
    y = (xf - mu) * lax.rsqrt(var + LN_EPS) * g.astype(jnp.float32) + b.astype(jnp.float32)
    return y.astype(x.dtype)


def _swiglu(x, w_in, w_out):
    gate, up = jnp.split(x @ w_in, 2, axis=-1)
    return (jax.nn.silu(gate) * up) @ w_out


def _rope_tables(seq, dh):
    pos = jnp.arange(seq, dtype=jnp.float32)
    inv = ROPE_THETA ** (-jnp.arange(0, dh, 2, dtype=jnp.float32) / dh)
    ang = pos[:, None] * inv[None, :]
    return jnp.cos(ang), jnp.sin(ang)


def _rope(x, cos, sin):
    xf = x.astype(jnp.float32)
    half = xf.shape[-1] // 2
    x1, x2 = xf[..., :half], xf[..., half:]
    c, s = cos[None, :, None, :], sin[None, :, None, :]
    return jnp.concatenate([x1 * c - x2 * s, x2 * c + x1 * s], axis=-1).astype(x.dtype)


def _banded_attention(q, k, v, max_dist, sinks):
    bsz, L, hq, dh = q.shape
    hk = k.shape[2]
    g = hq // hk
    lp = -(-L // BLK) * BLK
    pad = ((0, 0), (0, lp - L), (0, 0), (0, 0))
    nb = lp // BLK
    qb = jnp.pad(q, pad).astype(jnp.float32).reshape(bsz, nb, BLK, hk, g, dh)
    kb = jnp.pad(k, pad).astype(jnp.float32).reshape(bsz, nb, BLK, hk, dh)
    vb = jnp.pad(v, pad).astype(jnp.float32).reshape(bsz, nb, BLK, hk, dh)
    prev = ((0, 0), (1, 0), (0, 0), (0, 0), (0, 0))
    kk = jnp.concatenate([jnp.pad(kb, prev)[:, :-1], kb], axis=2)
    vv = jnp.concatenate([jnp.pad(vb, prev)[:, :-1], vb], axis=2)
    s = jnp.einsum('bnqhgd,bnkhd->bnhgqk', qb, kk) * (dh ** -0.5)
    qi = jnp.arange(BLK)[:, None]
    kj = jnp.arange(2 * BLK)[None, :]
    diff = BLK + qi - kj
    blk = jnp.arange(nb)[:, None, None]
    valid = (diff >= 0) & (diff <= max_dist) & ((blk > 0) | (kj >= BLK))
    s = jnp.where(valid[None, :, None, None], s, NEG_INF)
    m = s.max(-1)
    if sinks is not None:
        sink = sinks.astype(jnp.float32).reshape(hk, g)[None, None, :, :, None]
        m = jnp.maximum(m, sink)
        e = jnp.exp(s - m[..., None])
        den = e.sum(-1) + jnp.exp(sink - m)
    else:
        e = jnp.exp(s - m[..., None])
        den = e.sum(-1)
    lse = m + jnp.log(den)
    o = jnp.einsum('bnhgqk,bnkhd->bnqhgd', e / den[..., None], vv)
    o = o.reshape(bsz, lp, hq, dh)[:, :L]
    lse = lse.transpose(0, 1, 4, 2, 3).reshape(bsz, lp, hq)[:, :L]
    return o, lse


def _stick_breaking(q, k, v):
    bsz, S, H, dh = q.shape
    sp_len = -(-S // BLK) * BLK
    nb = sp_len // BLK
    qb = jnp.pad(q, ((0, 0), (0, sp_len - S), (0, 0), (0, 0))).astype(jnp.float32)
    qb = qb.reshape(bsz, nb, BLK, H, dh).transpose(1, 0, 3, 2, 4)
    kf = k.astype(jnp.float32)
    vf = v.astype(jnp.float32)
    kpos = jnp.arange(S)

    def block(args):
        qblk, n = args
        z = jnp.einsum('bhqd,bshd->bhqs', qblk, kf) * (dh ** -0.5)
        qpos = n * BLK + jnp.arange(BLK)
        causal = kpos[None, :] < qpos[:, None]
        sp = jnp.where(causal, jax.nn.softplus(z), 0.0)
        after = lax.cumsum(sp, axis=3, reverse=True) - sp
        a = jnp.where(causal, jnp.exp(jax.nn.log_sigmoid(z) - after), 0.0)
        return jnp.einsum('bhqs,bshd->bqhd', a, vf)

    o = lax.map(block, (qb, jnp.arange(nb)))
    return o.transpose(1, 0, 2, 3, 4).reshape(bsz, sp_len, H, dh)[:, :S].astype(q.dtype)


def _sb_mixer(x, w_in, w_out):
    bsz, S, _ = x.shape
    h = (x @ w_in).reshape(bsz, S, 3, SB_HEADS, SB_HEAD_DIM)
    o = _stick_breaking(h[:, :, 0], h[:, :, 1], h[:, :, 2])
    return o.reshape(bsz, S, SB_HEADS * SB_HEAD_DIM) @ w_out


def _swa_mixer(x, w_in, sinks, w_out, cos, sin):
    bsz, S, _ = x.shape
    nq = SWA_Q_HEADS * SWA_HEAD_DIM
    nkv = SWA_KV_HEADS * SWA_HEAD_DIM
    h = x @ w_in
    q = _rope(h[..., :nq].reshape(bsz, S, SWA_Q_HEADS, SWA_HEAD_DIM), cos, sin)
    k = _rope(h[..., nq:nq + nkv].reshape(bsz, S, SWA_KV_HEADS, SWA_HEAD_DIM), cos, sin)
    v = h[..., nq + nkv:].reshape(bsz, S, SWA_KV_HEADS, SWA_HEAD_DIM)
    o, _ = _banded_attention(q, k, v, SWA_WINDOW - 1, sinks)
    return o.reshape(bsz, S, nq).astype(x.dtype) @ w_out


def _dilated_mixer(x, w_in, w_out, cos, sin):
    bsz, S, _ = x.shape
    h = (x @ w_in).reshape(bsz, S, len(DIL_GROUPS), 3, DIL_HEADS, DIL_HEAD_DIM)
    outs, lses = [], []
    for gi, (win, dil) in enumerate(DIL_GROUPS):
        q = _rope(h[:, :, gi, 0], cos, sin)
        k = _rope(h[:, :, gi, 1], cos, sin)
        v = h[:, :, gi, 2]
        sp_len = -(-S // dil) * dil
        ls = sp_len // dil

        def strided(a):
            a = jnp.pad(a, ((0, 0), (0, sp_len - S), (0, 0), (0, 0)))
            return a.reshape(bsz, ls, dil, DIL_HEADS, DIL_HEAD_DIM).transpose(0, 2, 1, 3, 4).reshape(
                bsz * dil, ls, DIL_HEADS, DIL_HEAD_DIM)

        o, lse = _banded_attention(strided(q), strided(k), strided(v), win // dil, None)
        o = o.reshape(bsz, dil, ls, DIL_HEADS, DIL_HEAD_DIM).transpose(0, 2, 1, 3, 4).reshape(
            bsz, sp_len, DIL_HEADS, DIL_HEAD_DIM)[:, :S]
        lse = lse.reshape(bsz, dil, ls, DIL_HEADS).transpose(0, 2, 1, 3).reshape(bsz, sp_len, DIL_HEADS)[:, :S]
        outs.append(o)
        lses.append(lse)
    w = jax.nn.softmax(jnp.stack(lses), axis=0)
    o = jnp.einsum('gbsh,gbshd->bshd', w, jnp.stack(outs))
    return o.reshape(bsz, S, DIL_HEADS * DIL_HEAD_DIM).astype(x.dtype) @ w_out


def setup_inputs(seed: int = 0) -> dict:
    key = jax.random.key(seed)
    ks = jax.random.split(key, 18)
    f32 = jnp.float32

    def nrm(k, shape, scale):
        return jax.random.normal(k, shape, f32) * scale

    sb_w = SB_HEADS * SB_HEAD_DIM
    swa_in = (SWA_Q_HEADS + 2 * SWA_KV_HEADS) * SWA_HEAD_DIM
    swa_w = SWA_Q_HEADS * SWA_HEAD_DIM
    dil_in = len(DIL_GROUPS) * 3 * DIL_HEADS * DIL_HEAD_DIM
    dil_w = DIL_HEADS * DIL_HEAD_DIM
    return {
        'x': nrm(ks[0], (BATCH, SEQ, D_MODEL), 1.0),
        'p': nrm(ks[1], (DEPTH, BATCH, SEQ, PLE_DIM), 1.0),
        'ffn1_w_in': nrm(ks[2], (DEPTH, D_MODEL, 2 * D_FF), D_MODEL ** -0.5),
        'ffn1_w_out': nrm(ks[3], (DEPTH, D_FF, D_MODEL), D_FF ** -0.5 * DEEPNORM_BETA),
        'ffn2_w_in': nrm(ks[4], (DEPTH, D_MODEL, 2 * D_FF), D_MODEL ** -0.5),
        'ffn2_w_out': nrm(ks[5], (DEPTH, D_FF, D_MODEL), D_FF ** -0.5 * DEEPNORM_BETA),
        'ln_g': 1.0 + nrm(ks[6], (DEPTH, 3, D_MODEL), 0.02),
        'ln_b': nrm(ks[7], (DEPTH, 3, D_MODEL), 0.02),
        'sb_w_in': nrm(ks[8], (N_SB, D_MODEL, 3 * sb_w), D_MODEL ** -0.5),
        'sb_w_out': nrm(ks[9], (N_SB, sb_w, D_MODEL), sb_w ** -0.5 * DEEPNORM_BETA),
        'swa_w_in': nrm(ks[10], (N_SWA, D_MODEL, swa_in), D_MODEL ** -0.5),
        'swa_sinks': nrm(ks[11], (N_SWA, SWA_Q_HEADS), 0.5),
        'swa_w_out': nrm(ks[12], (N_SWA, swa_w, D_MODEL), swa_w ** -0.5 * DEEPNORM_BETA),
        'dil_w_in': nrm(ks[13], (N_DIL, D_MODEL, dil_in), D_MODEL ** -0.5),
        'dil_w_out': nrm(ks[14], (N_DIL, dil_w, D_MODEL), dil_w ** -0.5 * DEEPNORM_BETA),
        'ple_w_proj': nrm(ks[15], (DEPTH, PLE_DIM, D_MODEL), PLE_DIM ** -0.5),
        'ple_w_gate': nrm(ks[16], (DEPTH, D_MODEL, D_MODEL), D_MODEL ** -0.5),
    }


def reference(x, p, ffn1_w_in, ffn1_w_out, ffn2_w_in, ffn2_w_out, ln_g, ln_b,
              sb_w_in, sb_w_out, swa_w_in, swa_sinks, swa_w_out, dil_w_in, dil_w_out,
              ple_w_proj, ple_w_gate):
    S = x.shape[1]
    cos, sin = _rope_tables(S, SWA_HEAD_DIM)
    a = DEEPNORM_ALPHA
    for i in range(DEPTH):
        kind, j = i % N_MIXERS, i // N_MIXERS
        x = _layernorm(a * x + 0.5 * _swiglu(x, ffn1_w_in[i], ffn1_w_out[i]), ln_g[i, 0], ln_b[i, 0])
        if kind == 0:
            mix = _sb_mixer(x, sb_w_in[j], sb_w_out[j])
        elif kind == 1:
            mix = _swa_mixer(x, swa_w_in[j], swa_sinks[j], swa_w_out[j], cos, sin)
        else:
            mix = _dilated_mixer(x, dil_w_in[j], dil_w_out[j], cos, sin)
        x = _layernorm(a * x + mix, ln_g[i, 1], ln_b[i, 1])
        x = _layernorm(a * x + 0.5 * _swiglu(x, ffn2_w_in[i], ffn2_w_out[i]), ln_g[i, 2], ln_b[i, 2])
        x = x + jax.nn.sigmoid(x @ ple_w_gate[i]) * (p[i] @ ple_w_proj[i])
    return x
```

```python
import functools

import jax
import jax.numpy as jnp
from jax import lax
from jax.experimental import pallas as pl
from jax.experimental.pallas import tpu as pltpu

F32 = jnp.float32
BF16 = jnp.bfloat16

D_MODEL = 1024
D_FF = 2816
PLE_DIM = 256
LN_EPS = 1e-5
ROPE_THETA = 10000.0
N_MIXERS = 3
HEADS = 16
HEAD_DIM = 64
SWA_KV_HEADS = 2
SWA_WINDOW = 128
DIL_GROUPS = ((128, 1), (512, 4), (2048, 16))
BLK = 128
NEG_INF = -1e30

LANES = 128
PAIRS = HEADS // 2
FF_CHUNK = 256
N_FF_CHUNKS = D_FF // FF_CHUNK
VMEM_LIMIT = 56 * 1024 * 1024


def _cparams(*sem):
    return pltpu.CompilerParams(dimension_semantics=sem, vmem_limit_bytes=VMEM_LIMIT)


def _dot(a, b):
    return jnp.dot(a, b, preferred_element_type=F32)


def _dot_nt(a, b):
    return lax.dot_general(a, b, (((1,), (1,)), ((), ())), preferred_element_type=F32)


def _layernorm(y, g, b):
    mu = jnp.mean(y, axis=-1, keepdims=True)
    yc = y - mu
    var = jnp.mean(yc * yc, axis=-1, keepdims=True)
    return yc * lax.rsqrt(var + LN_EPS) * g + b


def _sigmoid(x):
    return 1.0 / (1.0 + jnp.exp(-x))


def _const_spec(shape, grid_rank):
    nd = len(shape)
    return pl.BlockSpec(shape, lambda *_: (0,) * nd, pipeline_mode=pl.Buffered(1))


def _ffn_body(x_ref, wg_ref, wu_ref, wo_ref, g_ref, b_ref, acc_ref, alpha):
    xb = x_ref[...].astype(BF16)
    acc_ref[...] = jnp.zeros_like(acc_ref)

    def chunk(c, carry):
        gate = _dot(xb, wg_ref[c])
        up = _dot(xb, wu_ref[c])
        h = (gate * _sigmoid(gate)) * up
        acc_ref[...] += _dot(h.astype(BF16), wo_ref[c])
        return carry

    lax.fori_loop(0, N_FF_CHUNKS, chunk, 0)
    y = alpha * x_ref[...] + 0.5 * acc_ref[...]
    return _layernorm(y, g_ref[...], b_ref[...])


def _ffn_kernel(x_ref, wg_ref, wu_ref, wo_ref, g_ref, b_ref, o_ref, acc_ref, *, alpha):
    o_ref[...] = _ffn_body(x_ref, wg_ref, wu_ref, wo_ref, g_ref, b_ref, acc_ref, alpha)


def _ffn_ple_kernel(x_ref, wg_ref, wu_ref, wo_ref, g_ref, b_ref, p_ref, pg_ref, pp_ref, o_ref, acc_ref,
                    *, alpha):
    x3 = _ffn_body(x_ref, wg_ref, wu_ref, wo_ref, g_ref, b_ref, acc_ref, alpha)
    gate = _dot(x3.astype(BF16), pg_ref[...])
    proj = _dot(p_ref[...].astype(BF16), pp_ref[...])
    o_ref[...] = x3 + _sigmoid(gate) * proj


def _ffn(x, wg3, wu3, wo3, g, b, alpha, ple=None, tm=512):
    t = x.shape[0]
    row = pl.BlockSpec((tm, D_MODEL), lambda i: (i, 0))
    in_specs = [row] + [_const_spec(a.shape, 1) for a in (wg3, wu3, wo3, g, b)]
    args = [x, wg3, wu3, wo3, g, b]
    if ple is None:
        body = functools.partial(_ffn_kernel, alpha=alpha)
    else:
        p, pg, pp = ple
        body = functools.partial(_ffn_ple_kernel, alpha=alpha)
        in_specs += [pl.BlockSpec((tm, PLE_DIM), lambda i: (i, 0)), _const_spec(pg.shape, 1),
                     _const_spec(pp.shape, 1)]
        args += [p, pg, pp]
    return pl.pallas_call(
        body,
        grid=(t // tm,),
        in_specs=in_specs,
        out_specs=row,
        out_shape=jax.ShapeDtypeStruct((t, D_MODEL), F32),
        scratch_shapes=[pltpu.VMEM((tm, D_MODEL), F32)],
        compiler_params=_cparams("parallel"),
        name="ffn_ple" if ple is not None else "ffn",
    )(*args)


def _rope_slab(a, cos, sin_signed):
    lane = lax.broadcasted_iota(jnp.int32, a.shape, 1)
    first_half = (lane % HEAD_DIM) < (HEAD_DIM // 2)
    partner = jnp.where(first_half, pltpu.roll(a, LANES - HEAD_DIM // 2, 1), pltpu.roll(a, HEAD_DIM // 2, 1))
    return a * cos + partner * sin_signed


def _proj_kernel(x_ref, w_ref, o_ref):
    o_ref[0, 0] = _dot(x_ref[0].astype(BF16), w_ref[...]).astype(o_ref.dtype)


def _proj_rope_kernel(x_ref, w_ref, cos_ref, sin_ref, o_ref, *, rope_pred):
    acc = _dot(x_ref[0].astype(BF16), w_ref[...])
    roped = rope_pred(pl.program_id(3))

    @pl.when(roped)
    def _():
        cos = cos_ref[...]
        sin_signed = sin_ref[...]
        for s in range(acc.shape[1] // LANES):
            a = acc[:, s * LANES:(s + 1) * LANES]
            o_ref[0, 0, :, s * LANES:(s + 1) * LANES] = _rope_slab(a, cos, sin_signed).astype(o_ref.dtype)

    @pl.when(jnp.logical_not(roped))
    def _():
        o_ref[0, 0] = acc.astype(o_ref.dtype)


def _proj(x, w, dil, tn, rope=None, tm=512):
    bsz, s, d = x.shape
    n = w.shape[1]
    ls = s // dil
    tm = min(tm, ls)
    xv = x.reshape(bsz, ls, dil * d)
    x_spec = pl.BlockSpec((1, tm, d), lambda b, r, i, j: (b, i, r))
    w_spec = pl.BlockSpec((d, tn), lambda b, r, i, j: (0, j))
    o_spec = pl.BlockSpec((1, 1, tm, tn), lambda b, r, i, j: (b, r, i, j))
    if rope is None:
        body, in_specs, args = _proj_kernel, [x_spec, w_spec], [xv, w]
    else:
        cos, sin_signed, rope_pred = rope
        tab_spec = pl.BlockSpec((tm, LANES), lambda b, r, i, j: (i, r))
        body = functools.partial(_proj_rope_kernel, rope_pred=rope_pred)
        in_specs = [x_spec, w_spec, tab_spec, tab_spec]
        args = [xv, w, cos.reshape(ls, dil * LANES), sin_signed.reshape(ls, dil * LANES)]
    return pl.pallas_call(
        body,
        grid=(bsz, dil, ls // tm, n // tn),
        in_specs=in_specs,
        out_specs=o_spec,
        out_shape=jax.ShapeDtypeStruct((bsz, dil, ls, n), BF16),
        compiler_params=_cparams("parallel", "parallel", "parallel", "arbitrary"),
        name="mixer_proj",
    )(*args)


def _sb_kernel(q_ref, k_ref, v_ref, o_ref, *, tq):
    i = pl.program_id(2)
    q2 = q_ref[0, 0]
    lane = lax.broadcasted_iota(jnp.int32, (1, LANES), 1)
    in_a = lane < HEAD_DIM
    zero = jnp.zeros_like(q2)
    q_heads = (jnp.where(in_a, q2, zero), jnp.where(in_a, zero, q2))
    row = lax.broadcasted_iota(jnp.int32, (tq, tq), 0)
    col = lax.broadcasted_iota(jnp.int32, (tq, tq), 1)
    suffix = (row >= col).astype(BF16)
    strict = col < row

    def block(j, carry, diag):
        acc, r_heads = carry[0], carry[1:]
        start = pl.multiple_of(j * tq, tq)
        kj = k_ref[0, 0, pl.ds(start, tq), :]
        vj = v_ref[0, 0, pl.ds(start, tq), :]
        pv, r_new = [], []
        for qh, r in zip(q_heads, r_heads):
            z = _dot_nt(qh, kj)
            sp = jnp.maximum(z, 0.0) + jnp.log1p(jnp.exp(-jnp.abs(z)))
            if diag:
                sp = jnp.where(strict, sp, 0.0)
            incl = _dot(sp.astype(BF16), suffix)
            a = jnp.exp(z - incl - r)
            if diag:
                a = jnp.where(strict, a, 0.0)
            pv.append(_dot(a.astype(BF16), vj))
            r_new.append(r + jnp.sum(sp, axis=1, keepdims=True))
        return (acc + jnp.where(in_a, pv[0], pv[1]), r_new[0], r_new[1])

    r0 = jnp.zeros((tq, 1), F32)
    carry = block(i, (jnp.zeros((tq, LANES), F32), r0, r0), True)
    carry = lax.fori_loop(0, i, lambda jj, c: block(i - 1 - jj, c, False), carry)
    o_ref[0] = carry[0].astype(o_ref.dtype)


def _sb_attention(h, tq=256):
    bsz, _, s, _ = h.shape
    return pl.pallas_call(
        functools.partial(_sb_kernel, tq=tq),
        grid=(bsz, PAIRS, s // tq),
        in_specs=[
            pl.BlockSpec((1, 1, tq, LANES), lambda b, p, i: (b, 0, i, p)),
            pl.BlockSpec((1, 1, s, LANES), lambda b, p, i: (b, 0, 0, PAIRS + p)),
            pl.BlockSpec((1, 1, s, LANES), lambda b, p, i: (b, 0, 0, 2 * PAIRS + p)),
        ],
        out_specs=pl.BlockSpec((1, tq, LANES), lambda b, p, i: (b, i, p)),
        out_shape=jax.ShapeDtypeStruct((bsz, s, D_MODEL), BF16),
        compiler_params=_cparams("parallel", "parallel", "arbitrary"),
        name="sb_attention",
    )(h, h, h)


def _band_kernel(*refs, max_dist, kv_slab, has_sink, with_lse):
    if has_sink:
        sink_ref, refs = refs[0], refs[1:]
    q_ref, kp_ref, kc_ref, vp_ref, vc_ref, o_ref = refs[:6]
    lse_ref = refs[6] if with_lse else None
    n = pl.program_id(2)
    qi = lax.broadcasted_iota(jnp.int32, (BLK, 2 * BLK), 0)
    kj = lax.broadcasted_iota(jnp.int32, (BLK, 2 * BLK), 1)
    diff = BLK + qi - kj
    valid = (diff >= 0) & (diff <= max_dist) & ((kj >= BLK) | (n > 0))
    lane = lax.broadcasted_iota(jnp.int32, (1, LANES), 1)
    in_a = lane < HEAD_DIM
    for p in range(PAIRS):
        q2 = q_ref[0, 0, :, p * LANES:(p + 1) * LANES]
        zero = jnp.zeros_like(q2)
        c0 = kv_slab(p) * LANES
        kk = jnp.concatenate([kp_ref[0, 0, :, c0:c0 + LANES], kc_ref[0, 0, :, c0:c0 + LANES]], axis=0)
        vv = jnp.concatenate([vp_ref[0, 0, :, c0:c0 + LANES], vc_ref[0, 0, :, c0:c0 + LANES]], axis=0)
        outs, lses = [], []
        for hh, qh in enumerate((jnp.where(in_a, q2, zero), jnp.where(in_a, zero, q2))):
            s = jnp.where(valid, _dot_nt(qh, kk), NEG_INF)
            m = jnp.max(s, axis=1, keepdims=True)
            if has_sink:
                sink = sink_ref[2 * p + hh]
                m = jnp.maximum(m, sink)
            e = jnp.exp(s - m)
            den = jnp.sum(e, axis=1, keepdims=True)
            if has_sink:
                den = den + jnp.exp(sink - m)
            outs.append(_dot(e.astype(BF16), vv) / den)
            lses.append(m + jnp.log(den))
        o_ref[0, 0, :, p * LANES:(p + 1) * LANES] = jnp.where(in_a, outs[0], outs[1]).astype(o_ref.dtype)
        if with_lse:
            lse_ref[0, 0, :, p * LANES:(p + 1) * LANES] = jnp.where(in_a, lses[0], lses[1])


def _band_attention(h, max_dist, q_blk, k_blk, v_blk, kv_width, kv_slab, sinks=None, with_lse=False,
                    out_dtype=BF16):
    bsz, dil, ls, _ = h.shape
    nb = ls // BLK
    cur = lambda blk: (lambda b, r, n: (b, r, n, blk))
    prev = lambda blk: (lambda b, r, n: (b, r, jnp.maximum(n - 1, 0), blk))
    kv_spec = lambda imap: pl.BlockSpec((1, 1, BLK, kv_width), imap)
    o_spec = pl.BlockSpec((1, 1, BLK, D_MODEL), lambda b, r, n: (b, r, n, 0))
    in_specs = [pl.BlockSpec((1, 1, BLK, D_MODEL), cur(q_blk)), kv_spec(prev(k_blk)), kv_spec(cur(k_blk)),
                kv_spec(prev(v_blk)), kv_spec(cur(v_blk))]
    args = [h, h, h, h, h]
    if sinks is not None:
        in_specs = [pl.BlockSpec(memory_space=pltpu.SMEM)] + in_specs
        args = [sinks] + args
    out_shape = [jax.ShapeDtypeStruct((bsz, dil, ls, D_MODEL), out_dtype)]
    out_specs = [o_spec]
    if with_lse:
        out_shape.append(jax.ShapeDtypeStruct((bsz, dil, ls, D_MODEL), F32))
        out_specs.append(o_spec)
    res = pl.pallas_call(
        functools.partial(_band_kernel, max_dist=max_dist, kv_slab=kv_slab, has_sink=sinks is not None,
                          with_lse=with_lse),
        grid=(bsz, dil, nb),
        in_specs=in_specs,
        out_specs=out_specs,
        out_shape=out_shape,
        compiler_params=_cparams("parallel", "parallel", "arbitrary"),
        name="band_attention",
    )(*args)
    return res if with_lse else res[0]


def _out_ln_kernel(o_ref, w_ref, x_ref, g_ref, b_ref, y_ref, *, alpha):
    mix = _dot(o_ref[...], w_ref[...])
    y_ref[...] = _layernorm(alpha * x_ref[...] + mix, g_ref[...], b_ref[...])


def _out_ln(o, w, x, g, b, alpha, tm=512):
    t = x.shape[0]
    row = pl.BlockSpec((tm, D_MODEL), lambda i: (i, 0))
    return pl.pallas_call(
        functools.partial(_out_ln_kernel, alpha=alpha),
        grid=(t // tm,),
        in_specs=[row, _const_spec(w.shape, 1), row, _const_spec(g.shape, 1), _const_spec(b.shape, 1)],
        out_specs=row,
        out_shape=jax.ShapeDtypeStruct((t, D_MODEL), F32),
        compiler_params=_cparams("parallel"),
        name="out_ln",
    )(o, w, x, g, b)


def _dil_out_ln_kernel(o0_ref, o1_ref, o2_ref, l0_ref, l1_ref, l2_ref, w_ref, x_ref, g_ref, b_ref, y_ref,
                       *, alpha):
    l0, l1, l2 = l0_ref[0], l1_ref[0, 0], l2_ref[0, 0]
    m = jnp.maximum(jnp.maximum(l0, l1), l2)
    e0, e1, e2 = jnp.exp(l0 - m), jnp.exp(l1 - m), jnp.exp(l2 - m)
    den = e0 + e1 + e2
    merged = (e0 * o0_ref[0] + e1 * o1_ref[0, 0] + e2 * o2_ref[0, 0]) / den
    mix = _dot(merged.astype(BF16), w_ref[...])
    y_ref[0] = _layernorm(alpha * x_ref[0] + mix, g_ref[...], b_ref[...])


def _dil_out_ln(outs, lses, w, x, g, b, alpha, tm=256):
    bsz, s, d = x.shape
    big = DIL_GROUPS[-1][1]
    mid = DIL_GROUPS[1][1]
    ls = s // big
    tm = min(tm, ls)
    v0 = lambda a: a.reshape(bsz, ls, big * d)
    v1 = lambda a: a.reshape(bsz, mid, ls, (big // mid) * d)
    s0 = pl.BlockSpec((1, tm, d), lambda b, r, i: (b, i, r))
    s1 = pl.BlockSpec((1, 1, tm, d), lambda b, r, i: (b, r % mid, i, r // mid))
    s2 = pl.BlockSpec((1, 1, tm, d), lambda b, r, i: (b, r, i, 0))
    y = pl.pallas_call(
        functools.partial(_dil_out_ln_kernel, alpha=alpha),
        grid=(bsz, big, ls // tm),
        in_specs=[s0, s1, s2, s0, s1, s2, _const_spec(w.shape, 3), s0, _const_spec(g.shape, 3),
                  _const_spec(b.shape, 3)],
        out_specs=s0,
        out_shape=jax.ShapeDtypeStruct((bsz, ls, big * d), F32),
        compiler_params=_cparams("parallel", "parallel", "parallel"),
        name="dil_out_ln",
    )(v0(outs[0]), v1(outs[1]), outs[2], v0(lses[0]), v1(lses[1]), lses[2], w, v0(x), g, b)
    return y.reshape(bsz, s, d)


def _ffn_weights(w_in, w_out):
    chunks = lambda w: w.reshape(D_MODEL, N_FF_CHUNKS, FF_CHUNK).transpose(1, 0, 2).astype(BF16)
    return chunks(w_in[:, :D_FF]), chunks(w_in[:, D_FF:]), w_out.reshape(N_FF_CHUNKS, FF_CHUNK, D_MODEL).astype(BF16)


def _rope_tables(seq):
    pos = jnp.arange(seq, dtype=F32)
    inv = ROPE_THETA ** (-jnp.arange(0, HEAD_DIM, 2, dtype=F32) / HEAD_DIM)
    ang = pos[:, None] * inv[None, :]
    cos, sin = jnp.cos(ang), jnp.sin(ang)
    reps = LANES // HEAD_DIM
    return jnp.tile(cos, (1, 2 * reps)), jnp.tile(jnp.concatenate([-sin, sin], axis=1), (1, reps))


def kernel(x, p, ffn1_w_in, ffn1_w_out, ffn2_w_in, ffn2_w_out, ln_g, ln_b, sb_w_in, sb_w_out, swa_w_in,
           swa_sinks, swa_w_out, dil_w_in, dil_w_out, ple_w_proj, ple_w_gate):
    bsz, seq, d = x.shape
    depth = p.shape[0]
    t = bsz * seq
    alpha = (2 * depth) ** 0.25
    q_scale = HEAD_DIM ** -0.5
    width = HEADS * HEAD_DIM
    cos, sin_signed = _rope_tables(seq)
    x = x.reshape(t, d)
    for i in range(depth):
        kind, j = i % N_MIXERS, i // N_MIXERS
        ln = lambda k: (ln_g[i, k].reshape(1, d), ln_b[i, k].reshape(1, d))
        x = _ffn(x, *_ffn_weights(ffn1_w_in[i], ffn1_w_out[i]), *ln(0), alpha)
        x3d = x.reshape(bsz, seq, d)
        if kind == 0:
            w = sb_w_in[j]
            w = jnp.concatenate([w[:, :width] * q_scale, w[:, width:]], axis=1).astype(BF16)
            h = _proj(x3d, w, 1, 1024)
            o = _sb_attention(h).reshape(t, width)
            x = _out_ln(o, sb_w_out[j].astype(BF16), x, *ln(1), alpha)
        elif kind == 1:
            w = swa_w_in[j]
            kvw = SWA_KV_HEADS * HEAD_DIM
            dup = lambda a: jnp.repeat(a.reshape(d, SWA_KV_HEADS, 1, HEAD_DIM), 2, axis=2).reshape(d, 2 * kvw)
            w = jnp.concatenate([w[:, :width] * q_scale, dup(w[:, width:width + kvw]), dup(w[:, width + kvw:])],
                                axis=1).astype(BF16)
            n_rope = (width + 2 * kvw) // (2 * kvw)
            h = _proj(x3d, w, 1, 2 * kvw, rope=(cos, sin_signed, lambda c: c < n_rope))
            o = _band_attention(h, SWA_WINDOW - 1, 0, width // (2 * kvw), width // (2 * kvw) + 1, 2 * kvw,
                                lambda pr: pr // (PAIRS // SWA_KV_HEADS), sinks=swa_sinks[j])
            x = _out_ln(o.reshape(t, width), swa_w_out[j].astype(BF16), x, *ln(1), alpha)
        else:
            outs, lses = [], []
            for gi, (win, dil) in enumerate(DIL_GROUPS):
                w = dil_w_in[j][:, gi * 3 * width:(gi + 1) * 3 * width]
                w = jnp.concatenate([w[:, :width] * q_scale, w[:, width:]], axis=1).astype(BF16)
                h = _proj(x3d, w, dil, width, rope=(cos, sin_signed, lambda c: c < 2))
                o, lse = _band_attention(h, win // dil, 0, 1, 2, width, lambda pr: pr, with_lse=True,
                                         out_dtype=F32)
                outs.append(o)
                lses.append(lse)
            x = _dil_out_ln(outs, lses, dil_w_out[j].astype(BF16), x3d, *ln(1), alpha).reshape(t, d)
        x = _ffn(x, *_ffn_weights(ffn2_w_in[i], ffn2_w_out[i]), *ln(2), alpha,
                 ple=(p[i].reshape(t, PLE_DIM), ple_w_gate[i].astype(BF16), ple_w_proj[i].astype(BF16)))
    return x.reshape(bsz, seq, d)
```

```python
import functools

import jax
import jax.numpy as jnp
from jax import lax
from jax.experimental import pallas as pl
from jax.experimental.pallas import tpu as pltpu

F32 = jnp.float32
BF16 = jnp.bfloat16

D_MODEL = 1024
D_FF = 2816
PLE_DIM = 256
LN_EPS = 1e-5
ROPE_THETA = 10000.0
N_MIXERS = 3
HEADS = 16
HEAD_DIM = 64
HALF_DIM = HEAD_DIM // 2
SWA_KV_HEADS = 2
SWA_WINDOW = 128
DIL_GROUPS = ((128, 1), (512, 4), (2048, 16))
BLK = 128
NEG_INF = -1e30
SB_DEAD_MASS = 110.0

LANES = 128
PAIRS = HEADS // 2
QUAD = 4
QUAD_W = QUAD * HEAD_DIM
N_QUADS = HEADS // QUAD
FF_CHUNK = 256
N_FF_CHUNKS = D_FF // FF_CHUNK
PROJ_CHUNK = 2 * QUAD_W
VMEM_LIMIT = 56 * 1024 * 1024


def _cparams(*sem):
    return pltpu.CompilerParams(dimension_semantics=sem, vmem_limit_bytes=VMEM_LIMIT)


def _dot(a, b):
    return jnp.dot(a, b, preferred_element_type=F32)


def _dot_nt(a, b):
    return lax.dot_general(a, b, (((1,), (1,)), ((), ())), preferred_element_type=F32)


def _layernorm(y, g, b):
    mu = jnp.mean(y, axis=-1, keepdims=True)
    yc = y - mu
    var = jnp.mean(yc * yc, axis=-1, keepdims=True)
    return yc * lax.rsqrt(var + LN_EPS) * g + b


def _sigmoid(x):
    return 1.0 / (1.0 + jnp.exp(-x))


def _const_spec(shape):
    nd = len(shape)
    return pl.BlockSpec(shape, lambda *_: (0,) * nd, pipeline_mode=pl.Buffered(1))


def _ffn_body(x_ref, wi_ref, wo_ref, g_ref, b_ref, h_ref, alpha):
    xb = x_ref[...].astype(BF16)
    for c in range(N_FF_CHUNKS):
        lo = c * FF_CHUNK
        gate = _dot(xb, wi_ref[:, lo:lo + FF_CHUNK])
        up = _dot(xb, wi_ref[:, D_FF + lo:D_FF + lo + FF_CHUNK])
        h_ref[:, lo:lo + FF_CHUNK] = ((gate * _sigmoid(gate)) * up).astype(BF16)
    y = alpha * x_ref[...] + _dot(h_ref[...], wo_ref[...])
    return _layernorm(y, g_ref[...], b_ref[...])


def _ffn_kernel(x_ref, wi_ref, wo_ref, g_ref, b_ref, o_ref, h_ref, *, alpha):
    o_ref[...] = _ffn_body(x_ref, wi_ref, wo_ref, g_ref, b_ref, h_ref, alpha)


def _ffn_ple_kernel(x_ref, wi_ref, wo_ref, g_ref, b_ref, p_ref, pg_ref, pp_ref, o_ref, h_ref, *, alpha):
    x3 = _ffn_body(x_ref, wi_ref, wo_ref, g_ref, b_ref, h_ref, alpha)
    gate = _dot(x3.astype(BF16), pg_ref[...])
    proj = _dot(p_ref[...].astype(BF16), pp_ref[...])
    o_ref[...] = x3 + _sigmoid(gate) * proj


def _ffn(x, wi, wo, g, b, alpha, ple=None, tm=512):
    t = x.shape[0]
    row = pl.BlockSpec((tm, D_MODEL), lambda i: (i, 0))
    in_specs = [row] + [_const_spec(a.shape) for a in (wi, wo, g, b)]
    args = [x, wi, wo, g, b]
    if ple is None:
        body = functools.partial(_ffn_kernel, alpha=alpha)
    else:
        p, pg, pp = ple
        body = functools.partial(_ffn_ple_kernel, alpha=alpha)
        in_specs += [pl.BlockSpec((tm, PLE_DIM), lambda i: (i, 0)), _const_spec(pg.shape), _const_spec(pp.shape)]
        args += [p, pg, pp]
    return pl.pallas_call(
        body,
        grid=(t // tm,),
        in_specs=in_specs,
        out_specs=row,
        out_shape=jax.ShapeDtypeStruct((t, D_MODEL), F32),
        scratch_shapes=[pltpu.VMEM((tm, D_FF), BF16)],
        compiler_params=_cparams("parallel"),
        name="ffn_ple" if ple is not None else "ffn",
    )(*args)


def _proj_kernel(*refs, roped):
    if any(roped):
        x_ref, w_ref, cos_ref, sin_ref, o_ref = refs
        cos, sin = cos_ref[...], sin_ref[...]
    else:
        x_ref, w_ref, o_ref = refs
    xb = x_ref[0].astype(BF16)
    per_chunk = PROJ_CHUNK // QUAD_W
    for c in range(len(roped) // per_chunk):
        acc = _dot(xb, w_ref[:, c * PROJ_CHUNK:(c + 1) * PROJ_CHUNK])
        for k in range(per_chunk):
            lo = k * QUAD_W
            col = c * PROJ_CHUNK + lo
            if roped[c * per_chunk + k]:
                x1, x2 = acc[:, lo:lo + LANES], acc[:, lo + LANES:lo + QUAD_W]
                o_ref[0, 0, :, col:col + LANES] = (x1 * cos - x2 * sin).astype(o_ref.dtype)
                o_ref[0, 0, :, col + LANES:col + QUAD_W] = (x2 * cos + x1 * sin).astype(o_ref.dtype)
            else:
                o_ref[0, 0, :, col:col + QUAD_W] = acc[:, lo:lo + QUAD_W].astype(o_ref.dtype)


def _proj(x, w, dil, roped, tables=None, tm=512):
    bsz, s, d = x.shape
    n = w.shape[1]
    ls = s // dil
    tm = min(tm, ls)
    in_specs = [pl.BlockSpec((1, tm, d), lambda b, r, i: (b, i, r)), _const_spec(w.shape)]
    args = [x.reshape(bsz, ls, dil * d), w]
    if any(roped):
        tab_spec = pl.BlockSpec((tm, LANES), lambda b, r, i: (i, r))
        in_specs += [tab_spec, tab_spec]
        args += [tab.reshape(ls, dil * LANES) for tab in tables]
    return pl.pallas_call(
        functools.partial(_proj_kernel, roped=tuple(roped)),
        grid=(bsz, dil, ls // tm),
        in_specs=in_specs,
        out_specs=pl.BlockSpec((1, 1, tm, n), lambda b, r, i: (b, r, i, 0)),
        out_shape=jax.ShapeDtypeStruct((bsz, dil, ls, n), BF16),
        compiler_params=_cparams("parallel", "parallel", "parallel"),
        name="mixer_proj",
    )(*args)


def _sb_kernel(q_ref, k_ref, v_ref, o_ref, *, tq):
    i = pl.program_id(2)
    q2 = q_ref[0, 0]
    lane = lax.broadcasted_iota(jnp.int32, (1, LANES), 1)
    in_a = lane < HEAD_DIM
    zero = jnp.zeros_like(q2)
    qs = jnp.concatenate([jnp.where(in_a, q2, zero), jnp.where(in_a, zero, q2)], axis=0)
    row = lax.broadcasted_iota(jnp.int32, (tq, tq), 0)
    col = lax.broadcasted_iota(jnp.int32, (tq, tq), 1)
    suffix = (row >= col).astype(BF16)
    row2 = lax.broadcasted_iota(jnp.int32, (2 * tq, tq), 0) % tq
    col2 = lax.broadcasted_iota(jnp.int32, (2 * tq, tq), 1)
    strict = col2 < row2

    def block(j, carry, diag):
        acc, r = carry
        start = pl.multiple_of(j * tq, tq)
        kj = k_ref[0, 0, pl.ds(start, tq), :]
        vj = v_ref[0, 0, pl.ds(start, tq), :]
        z = _dot_nt(qs, kj)
        sp = jnp.maximum(z, 0.0) + jnp.log(1.0 + jnp.exp(-jnp.abs(z)))
        if diag:
            sp = jnp.where(strict, sp, 0.0)
        incl = _dot(sp.astype(BF16), suffix)
        a = jnp.exp(z - incl - r)
        if diag:
            a = jnp.where(strict, a, 0.0)
        pv = _dot(a.astype(BF16), vj)
        return (acc + jnp.where(in_a, pv[:tq], pv[tq:]), r + jnp.sum(sp, axis=1, keepdims=True))

    def more(state):
        j, mass = state[0], state[1]
        return jnp.logical_and(j >= 0, mass < SB_DEAD_MASS)

    def step(state):
        c = block(state[0], state[2:], False)
        return (state[0] - 1, jnp.min(c[1])) + c

    carry = block(i, (jnp.zeros((tq, LANES), F32), jnp.zeros((2 * tq, 1), F32)), True)
    state = lax.while_loop(more, step, (i - 1, jnp.min(carry[1])) + carry)
    o_ref[0] = state[2].astype(o_ref.dtype)


def _sb_attention(h, tq=256):
    bsz, _, s, _ = h.shape
    return pl.pallas_call(
        functools.partial(_sb_kernel, tq=tq),
        grid=(bsz, PAIRS, s // tq),
        in_specs=[
            pl.BlockSpec((1, 1, tq, LANES), lambda b, p, i: (b, 0, i, p)),
            pl.BlockSpec((1, 1, s, LANES), lambda b, p, i: (b, 0, 0, PAIRS + p)),
            pl.BlockSpec((1, 1, s, LANES), lambda b, p, i: (b, 0, 0, 2 * PAIRS + p)),
        ],
        out_specs=pl.BlockSpec((1, tq, LANES), lambda b, p, i: (b, i, p)),
        out_shape=jax.ShapeDtypeStruct((bsz, s, D_MODEL), BF16),
        compiler_params=_cparams("parallel", "parallel", "arbitrary"),
        name="sb_attention",
    )(h, h, h)


def _band_kernel(*refs, max_dist, kv_quad, has_sink, with_lse):
    if has_sink:
        sink_ref, refs = refs[0], refs[1:]
    q_ref, kp_ref, kc_ref, vp_ref, vc_ref, o_ref = refs[:6]
    lse_ref = refs[6] if with_lse else None
    n = pl.program_id(2)
    row = lax.broadcasted_iota(jnp.int32, (QUAD * BLK, 2 * BLK), 0)
    kj = lax.broadcasted_iota(jnp.int32, (QUAD * BLK, 2 * BLK), 1)
    diff = BLK + row % BLK - kj
    valid = (diff >= 0) & (diff <= max_dist) & ((kj >= BLK) | (n > 0))
    lane = lax.broadcasted_iota(jnp.int32, (1, QUAD_W), 1)
    qk_sel = [(lane % LANES) // HALF_DIM == hq for hq in range(QUAD)]
    in_a = lax.broadcasted_iota(jnp.int32, (1, LANES), 1) < HEAD_DIM
    head_of_row = lax.broadcasted_iota(jnp.int32, (QUAD * BLK, 1), 0) // BLK
    for g in range(N_QUADS):
        q4 = q_ref[0, 0, :, g * QUAD_W:(g + 1) * QUAD_W]
        zero = jnp.zeros_like(q4)
        c0 = kv_quad(g) * QUAD_W
        kk = jnp.concatenate([kp_ref[0, 0, :, c0:c0 + QUAD_W], kc_ref[0, 0, :, c0:c0 + QUAD_W]], axis=0)
        qs = jnp.concatenate([jnp.where(qk_sel[hq], q4, zero) for hq in range(QUAD)], axis=0)
        s = jnp.where(valid, _dot_nt(qs, kk), NEG_INF)
        m = jnp.max(s, axis=1, keepdims=True)
        if has_sink:
            sink = jnp.full((QUAD * BLK, 1), sink_ref[QUAD * g], F32)
            for hq in range(1, QUAD):
                sink = jnp.where(head_of_row == hq, sink_ref[QUAD * g + hq], sink)
            m = jnp.maximum(m, sink)
        e = jnp.exp(s - m)
        den = jnp.sum(e, axis=1, keepdims=True)
        if has_sink:
            den = den + jnp.exp(sink - m)
        eb = e.astype(BF16)
        lse = m + jnp.log(den)
        for pr in range(QUAD // 2):
            c1 = c0 + pr * LANES
            rows = slice(2 * pr * BLK, 2 * (pr + 1) * BLK)
            vv = jnp.concatenate([vp_ref[0, 0, :, c1:c1 + LANES], vc_ref[0, 0, :, c1:c1 + LANES]], axis=0)
            o2 = _dot(eb[rows], vv) / den[rows]
            col = g * QUAD_W + pr * LANES
            o_ref[0, 0, :, col:col + LANES] = jnp.where(in_a, o2[:BLK], o2[BLK:]).astype(o_ref.dtype)
            if with_lse:
                l2 = lse[rows]
                lse_ref[0, 0, :, col:col + LANES] = jnp.where(in_a, l2[:BLK], l2[BLK:])


def _band_attention(h, max_dist, kv_width, kv_quad, sinks=None, with_lse=False, out_dtype=BF16):
    bsz, dil, ls, _ = h.shape
    k_blk = D_MODEL // kv_width
    cur = lambda blk: (lambda b, r, n: (b, r, n, blk))
    prev = lambda blk: (lambda b, r, n: (b, r, jnp.maximum(n - 1, 0), blk))
    kv_spec = lambda imap: pl.BlockSpec((1, 1, BLK, kv_width), imap)
    o_spec = pl.BlockSpec((1, 1, BLK, D_MODEL), lambda b, r, n: (b, r, n, 0))
    in_specs = [pl.BlockSpec((1, 1, BLK, D_MODEL), cur(0)), kv_spec(prev(k_blk)), kv_spec(cur(k_blk)),
                kv_spec(prev(k_blk + 1)), kv_spec(cur(k_blk + 1))]
    args = [h, h, h, h, h]
    if sinks is not None:
        in_specs = [pl.BlockSpec(memory_space=pltpu.SMEM)] + in_specs
        args = [sinks] + args
    out_shape = [jax.ShapeDtypeStruct((bsz, dil, ls, D_MODEL), out_dtype)]
    out_specs = [o_spec]
    if with_lse:
        out_shape.append(jax.ShapeDtypeStruct((bsz, dil, ls, D_MODEL), F32))
        out_specs.append(o_spec)
    res = pl.pallas_call(
        functools.partial(_band_kernel, max_dist=max_dist, kv_quad=kv_quad, has_sink=sinks is not None,
                          with_lse=with_lse),
        grid=(bsz, dil, ls // BLK),
        in_specs=in_specs,
        out_specs=out_specs,
        out_shape=out_shape,
        compiler_params=_cparams("parallel", "parallel", "arbitrary"),
        name="band_attention",
    )(*args)
    return res if with_lse else res[0]


def _out_ln_kernel(o_ref, w_ref, x_ref, g_ref, b_ref, y_ref, *, alpha):
    mix = _dot(o_ref[...], w_ref[...])
    y_ref[...] = _layernorm(alpha * x_ref[...] + mix, g_ref[...], b_ref[...])


def _out_ln(o, w, x, g, b, alpha, tm=512):
    t = x.shape[0]
    row = pl.BlockSpec((tm, D_MODEL), lambda i: (i, 0))
    return pl.pallas_call(
        functools.partial(_out_ln_kernel, alpha=alpha),
        grid=(t // tm,),
        in_specs=[row, _const_spec(w.shape), row, _const_spec(g.shape), _const_spec(b.shape)],
        out_specs=row,
        out_shape=jax.ShapeDtypeStruct((t, D_MODEL), F32),
        compiler_params=_cparams("parallel"),
        name="out_ln",
    )(o, w, x, g, b)


def _dil_out_ln_kernel(o0_ref, o1_ref, o2_ref, l0_ref, l1_ref, l2_ref, w_ref, x_ref, g_ref, b_ref, y_ref,
                       *, alpha):
    l0, l1, l2 = l0_ref[0], l1_ref[0, 0], l2_ref[0, 0]
    m = jnp.maximum(jnp.maximum(l0, l1), l2)
    e0, e1, e2 = jnp.exp(l0 - m), jnp.exp(l1 - m), jnp.exp(l2 - m)
    den = e0 + e1 + e2
    merged = (e0 * o0_ref[0] + e1 * o1_ref[0, 0] + e2 * o2_ref[0, 0]) / den
    mix = _dot(merged.astype(BF16), w_ref[...])
    y_ref[0] = _layernorm(alpha * x_ref[0] + mix, g_ref[...], b_ref[...])


def _dil_out_ln(outs, lses, w, x, g, b, alpha, tm=256):
    bsz, s, d = x.shape
    big = DIL_GROUPS[-1][1]
    mid = DIL_GROUPS[1][1]
    ls = s // big
    tm = min(tm, ls)
    v0 = lambda a: a.reshape(bsz, ls, big * d)
    v1 = lambda a: a.reshape(bsz, mid, ls, (big // mid) * d)
    s0 = pl.BlockSpec((1, tm, d), lambda b, r, i: (b, i, r))
    s1 = pl.BlockSpec((1, 1, tm, d), lambda b, r, i: (b, r % mid, i, r // mid))
    s2 = pl.BlockSpec((1, 1, tm, d), lambda b, r, i: (b, r, i, 0))
    y = pl.pallas_call(
        functools.partial(_dil_out_ln_kernel, alpha=alpha),
        grid=(bsz, big, ls // tm),
        in_specs=[s0, s1, s2, s0, s1, s2, _const_spec(w.shape), s0, _const_spec(g.shape), _const_spec(b.shape)],
        out_specs=s0,
        out_shape=jax.ShapeDtypeStruct((bsz, ls, big * d), F32),
        compiler_params=_cparams("parallel", "parallel", "parallel"),
        name="dil_out_ln",
    )(v0(outs[0]), v1(outs[1]), outs[2], v0(lses[0]), v1(lses[1]), lses[2], w, v0(x), g, b)
    return y.reshape(bsz, s, d)


def _quad_cols(w):
    d, n = w.shape
    w = w.reshape(d, n // QUAD_W, QUAD, 2, HALF_DIM).transpose(0, 1, 3, 2, 4)
    return w.reshape(d, n)


def _rope_tables(seq):
    pos = jnp.arange(seq, dtype=F32)
    inv = ROPE_THETA ** (-jnp.arange(0, HEAD_DIM, 2, dtype=F32) / HEAD_DIM)
    ang = pos[:, None] * inv[None, :]
    reps = LANES // HALF_DIM
    return jnp.tile(jnp.cos(ang), (1, reps)), jnp.tile(jnp.sin(ang), (1, reps))


def kernel(x, p, ffn1_w_in, ffn1_w_out, ffn2_w_in, ffn2_w_out, ln_g, ln_b, sb_w_in, sb_w_out, swa_w_in,
           swa_sinks, swa_w_out, dil_w_in, dil_w_out, ple_w_proj, ple_w_gate):
    bsz, seq, d = x.shape
    depth = p.shape[0]
    t = bsz * seq
    alpha = (2 * depth) ** 0.25
    q_scale = HEAD_DIM ** -0.5
    width = HEADS * HEAD_DIM
    tables = _rope_tables(seq)
    x = x.reshape(t, d)
    for i in range(depth):
        kind, j = i % N_MIXERS, i // N_MIXERS
        ln = lambda k: (ln_g[i, k].reshape(1, d), ln_b[i, k].reshape(1, d))
        x = _ffn(x, ffn1_w_in[i].astype(BF16), (0.5 * ffn1_w_out[i]).astype(BF16), *ln(0), alpha)
        x3d = x.reshape(bsz, seq, d)
        if kind == 0:
            w = sb_w_in[j]
            w = jnp.concatenate([w[:, :width] * q_scale, w[:, width:]], axis=1).astype(BF16)
            h = _proj(x3d, w, 1, (False,) * (w.shape[1] // QUAD_W))
            o = _sb_attention(h).reshape(t, width)
            x = _out_ln(o, sb_w_out[j].astype(BF16), x, *ln(1), alpha)
        elif kind == 1:
            w = swa_w_in[j]
            kvw = SWA_KV_HEADS * HEAD_DIM
            wk = w[:, width:width + kvw].reshape(d, SWA_KV_HEADS, 2, 1, HALF_DIM)
            wv = w[:, width + kvw:].reshape(d, SWA_KV_HEADS, 1, HEAD_DIM)
            w = jnp.concatenate([_quad_cols(w[:, :width] * q_scale),
                                 jnp.repeat(wk, QUAD, axis=3).reshape(d, SWA_KV_HEADS * QUAD_W),
                                 jnp.repeat(wv, QUAD, axis=2).reshape(d, SWA_KV_HEADS * QUAD_W)],
                                axis=1).astype(BF16)
            roped = (True,) * (N_QUADS + SWA_KV_HEADS) + (False,) * SWA_KV_HEADS
            h = _proj(x3d, w, 1, roped, tables)
            o = _band_attention(h, SWA_WINDOW - 1, SWA_KV_HEADS * QUAD_W, lambda g: g // (N_QUADS // SWA_KV_HEADS),
                                sinks=swa_sinks[j])
            x = _out_ln(o.reshape(t, width), swa_w_out[j].astype(BF16), x, *ln(1), alpha)
        else:
            outs, lses = [], []
            for gi, (win, dil) in enumerate(DIL_GROUPS):
                w = dil_w_in[j][:, gi * 3 * width:(gi + 1) * 3 * width]
                w = jnp.concatenate([_quad_cols(w[:, :width] * q_scale), _quad_cols(w[:, width:2 * width]),
                                     w[:, 2 * width:]], axis=1).astype(BF16)
                h = _proj(x3d, w, dil, (True,) * (2 * N_QUADS) + (False,) * N_QUADS, tables)
                o, lse = _band_attention(h, win // dil, width, lambda g: g, with_lse=True, out_dtype=F32)
                outs.append(o)
                lses.append(lse)
            x = _dil_out_ln(outs, lses, dil_w_out[j].astype(BF16), x3d, *ln(1), alpha).reshape(t, d)
        x = _ffn(x, ffn2_w_in[i].astype(BF16), (0.5 * ffn2_w_out[i]).astype(BF16), *ln(2), alpha,
                 ple=(p[i].reshape(t, PLE_DIM), ple_w_gate[i].astype(BF16), ple_w_proj[i].astype(BF16)))
    return x.reshape(bsz, seq, d)
```

```python
import functools

import jax
import jax.numpy as jnp
from jax import lax
from jax.experimental import pallas as pl
from jax.experimental.pallas import tpu as pltpu

F32 = jnp.float32
BF16 = jnp.bfloat16

D_MODEL = 1024
D_FF = 2816
PLE_DIM = 256
LN_EPS = 1e-5
ROPE_THETA = 10000.0
N_MIXERS = 3
HEADS = 16
HEAD_DIM = 64
HALF_DIM = HEAD_DIM // 2
SWA_KV_HEADS = 2
SWA_WINDOW = 128
DIL_GROUPS = ((128, 1), (512, 4), (2048, 16))
BLK = 128
NEG_INF = -1e30
SB_DEAD_MASS = 110.0

LANES = 128
PAIRS = HEADS // 2
QUAD = 4
QUAD_W = QUAD * HEAD_DIM
N_QUADS = HEADS // QUAD
FF_CHUNK = 256
N_FF_CHUNKS = D_FF // FF_CHUNK
PROJ_CHUNK = 2 * QUAD_W
VMEM_LIMIT = 56 * 1024 * 1024


def _cparams(*sem):
    return pltpu.CompilerParams(dimension_semantics=sem, vmem_limit_bytes=VMEM_LIMIT)


def _dot(a, b):
    return jnp.dot(a, b, preferred_element_type=F32)


def _dot_nt(a, b):
    return lax.dot_general(a, b, (((1,), (1,)), ((), ())), preferred_element_type=F32)


def _layernorm(y, g, b):
    mu = jnp.mean(y, axis=-1, keepdims=True)
    yc = y - mu
    var = jnp.mean(yc * yc, axis=-1, keepdims=True)
    return yc * lax.rsqrt(var + LN_EPS) * g + b


def _sigmoid(x):
    return 1.0 / (1.0 + jnp.exp(-x))


def _const_spec(shape):
    nd = len(shape)
    return pl.BlockSpec(shape, lambda *_: (0,) * nd, pipeline_mode=pl.Buffered(1))


def _layer_spec(stack, layer):
    return pl.BlockSpec((None,) + stack.shape[1:], lambda *_: (layer, 0, 0), pipeline_mode=pl.Buffered(1))


def _cast_kernel(w_ref, o_ref, *, scale):
    w = w_ref[...]
    if scale != 1.0:
        w = w * scale
    o_ref[...] = w.astype(o_ref.dtype)


def _cast_bf16(w, scale=1.0, rows=256):
    n_layers, r, c = w.shape
    rb = min(rows, r)
    spec = pl.BlockSpec((1, rb, c), lambda a, b: (a, b, 0))
    return pl.pallas_call(
        functools.partial(_cast_kernel, scale=scale),
        grid=(n_layers, r // rb),
        in_specs=[spec],
        out_specs=spec,
        out_shape=jax.ShapeDtypeStruct(w.shape, BF16),
        compiler_params=_cparams("parallel", "parallel"),
        name="cast_bf16",
    )(w)


def _ffn_body(x_ref, wi_ref, wo_ref, g_ref, b_ref, h_ref, alpha):
    xb = x_ref[...].astype(BF16)
    for c in range(N_FF_CHUNKS):
        lo = c * FF_CHUNK
        gate = _dot(xb, wi_ref[:, lo:lo + FF_CHUNK])
        up = _dot(xb, wi_ref[:, D_FF + lo:D_FF + lo + FF_CHUNK])
        h_ref[:, lo:lo + FF_CHUNK] = ((gate * _sigmoid(gate)) * up).astype(BF16)
    y = alpha * x_ref[...] + _dot(h_ref[...], wo_ref[...])
    return _layernorm(y, g_ref[...], b_ref[...])


def _ffn_kernel(x_ref, wi_ref, wo_ref, g_ref, b_ref, o_ref, h_ref, *, alpha):
    o_ref[...] = _ffn_body(x_ref, wi_ref, wo_ref, g_ref, b_ref, h_ref, alpha)


def _ffn_ple_kernel(x_ref, wi_ref, wo_ref, g_ref, b_ref, p_ref, pg_ref, pp_ref, o_ref, h_ref, *, alpha):
    x3 = _ffn_body(x_ref, wi_ref, wo_ref, g_ref, b_ref, h_ref, alpha)
    gate = _dot(x3.astype(BF16), pg_ref[...])
    proj = _dot(p_ref[...].astype(BF16), pp_ref[...])
    o_ref[...] = x3 + _sigmoid(gate) * proj


def _ffn(x, layer, wi, wo, g, b, alpha, ple=None, tm=512):
    t = x.shape[0]
    row = pl.BlockSpec((tm, D_MODEL), lambda i: (i, 0))
    in_specs = [row, _layer_spec(wi, layer), _layer_spec(wo, layer), _const_spec(g.shape), _const_spec(b.shape)]
    args = [x, wi, wo, g, b]
    if ple is None:
        body = functools.partial(_ffn_kernel, alpha=alpha)
    else:
        p, pg, pp = ple
        body = functools.partial(_ffn_ple_kernel, alpha=alpha)
        in_specs += [pl.BlockSpec((None, tm, PLE_DIM), lambda i: (layer, i, 0)), _layer_spec(pg, layer),
                     _layer_spec(pp, layer)]
        args += [p, pg, pp]
    return pl.pallas_call(
        body,
        grid=(t // tm,),
        in_specs=in_specs,
        out_specs=row,
        out_shape=jax.ShapeDtypeStruct((t, D_MODEL), F32),
        scratch_shapes=[pltpu.VMEM((tm, D_FF), BF16)],
        compiler_params=_cparams("parallel"),
        name="ffn_ple" if ple is not None else "ffn",
    )(*args)


def _proj_kernel(*refs, roped):
    if any(roped):
        x_ref, w_ref, cos_ref, sin_ref, o_ref = refs
        cos, sin = cos_ref[...], sin_ref[...]
    else:
        x_ref, w_ref, o_ref = refs
    xb = x_ref[0].astype(BF16)
    per_chunk = PROJ_CHUNK // QUAD_W
    for c in range(len(roped) // per_chunk):
        acc = _dot(xb, w_ref[:, c * PROJ_CHUNK:(c + 1) * PROJ_CHUNK])
        for k in range(per_chunk):
            lo = k * QUAD_W
            col = c * PROJ_CHUNK + lo
            if roped[c * per_chunk + k]:
                x1, x2 = acc[:, lo:lo + LANES], acc[:, lo + LANES:lo + QUAD_W]
                o_ref[0, 0, :, col:col + LANES] = (x1 * cos - x2 * sin).astype(o_ref.dtype)
                o_ref[0, 0, :, col + LANES:col + QUAD_W] = (x2 * cos + x1 * sin).astype(o_ref.dtype)
            else:
                o_ref[0, 0, :, col:col + QUAD_W] = acc[:, lo:lo + QUAD_W].astype(o_ref.dtype)


def _proj(x, w, dil, roped, tables=None, tm=512):
    bsz, s, d = x.shape
    n = w.shape[1]
    ls = s // dil
    tm = min(tm, ls)
    in_specs = [pl.BlockSpec((1, tm, d), lambda b, r, i: (b, i, r)), _const_spec(w.shape)]
    args = [x.reshape(bsz, ls, dil * d), w]
    if any(roped):
        tab_spec = pl.BlockSpec((tm, LANES), lambda b, r, i: (i, r))
        in_specs += [tab_spec, tab_spec]
        args += [tab.reshape(ls, dil * LANES) for tab in tables]
    return pl.pallas_call(
        functools.partial(_proj_kernel, roped=tuple(roped)),
        grid=(bsz, dil, ls // tm),
        in_specs=in_specs,
        out_specs=pl.BlockSpec((1, 1, tm, n), lambda b, r, i: (b, r, i, 0)),
        out_shape=jax.ShapeDtypeStruct((bsz, dil, ls, n), BF16),
        compiler_params=_cparams("parallel", "parallel", "parallel"),
        name="mixer_proj",
    )(*args)


def _sb_kernel(q_ref, k_ref, v_ref, o_ref, acc_ref, r_ref, *, tq):
    i = pl.program_id(1)
    has_prev = i > 0
    d_start = pl.multiple_of(i * tq, tq)
    p_start = pl.multiple_of(jnp.maximum(i - 1, 0) * tq, tq)
    in_a = lax.broadcasted_iota(jnp.int32, (1, LANES), 1) < HEAD_DIM
    row = lax.broadcasted_iota(jnp.int32, (tq, tq), 0)
    col = lax.broadcasted_iota(jnp.int32, (tq, tq), 1)
    suffix = (row >= col).astype(BF16)
    row2 = lax.broadcasted_iota(jnp.int32, (2 * tq, tq), 0) % tq
    col2 = lax.broadcasted_iota(jnp.int32, (2 * tq, tq), 1)
    strict = col2 < row2

    def stacked_q(sl):
        q2 = q_ref[0, 0, :, sl]
        zero = jnp.zeros_like(q2)
        return jnp.concatenate([jnp.where(in_a, q2, zero), jnp.where(in_a, zero, q2)], axis=0)

    def softplus(z):
        return jnp.maximum(z, 0.0) + jnp.log(1.0 + jnp.exp(-jnp.abs(z)))

    def fold(pv):
        return jnp.where(in_a, pv[:tq], pv[tq:])

    for p in range(PAIRS):
        sl = slice(p * LANES, (p + 1) * LANES)
        qs = stacked_q(sl)
        z = _dot_nt(qs, k_ref[0, 0, pl.ds(d_start, tq), sl])
        sp = jnp.where(strict, softplus(z), 0.0)
        a = jnp.where(strict, jnp.exp(z - _dot(sp.astype(BF16), suffix)), 0.0)
        pv = _dot(a.astype(BF16), v_ref[0, 0, pl.ds(d_start, tq), sl])
        r = jnp.sum(sp, axis=1, keepdims=True)
        z = _dot_nt(qs, k_ref[0, 0, pl.ds(p_start, tq), sl])
        sp = softplus(z)
        a = jnp.where(has_prev, jnp.exp(z - _dot(sp.astype(BF16), suffix) - r), 0.0)
        pv = pv + _dot(a.astype(BF16), v_ref[0, 0, pl.ds(p_start, tq), sl])
        acc_ref[:, sl] = fold(pv)
        r_ref[p] = r + jnp.where(has_prev, jnp.sum(sp, axis=1, keepdims=True), 0.0)

    def more(state):
        j, mass = state[0], state[1]
        return jnp.logical_and(j >= 0, mass < SB_DEAD_MASS)

    for p in range(PAIRS):
        sl = slice(p * LANES, (p + 1) * LANES)

        def step(state, sl=sl):
            j, _, acc, r = state
            start = pl.multiple_of(j * tq, tq)
            z = _dot_nt(stacked_q(sl), k_ref[0, 0, pl.ds(start, tq), sl])
            sp = softplus(z)
            a = jnp.exp(z - _dot(sp.astype(BF16), suffix) - r)
            acc = acc + fold(_dot(a.astype(BF16), v_ref[0, 0, pl.ds(start, tq), sl]))
            r = r + jnp.sum(sp, axis=1, keepdims=True)
            return (j - 1, jnp.min(r), acc, r)

        r = r_ref[p]
        state = lax.while_loop(more, step, (i - 2, jnp.min(r), acc_ref[:, sl], r))
        o_ref[0, :, sl] = state[2].astype(o_ref.dtype)


def _sb_attention(h, tq=256):
    bsz, _, s, _ = h.shape
    whole = lambda blk: pl.BlockSpec((1, 1, s, D_MODEL), lambda b, i: (b, 0, 0, blk), pipeline_mode=pl.Buffered(1))
    return pl.pallas_call(
        functools.partial(_sb_kernel, tq=tq),
        grid=(bsz, s // tq),
        in_specs=[pl.BlockSpec((1, 1, tq, D_MODEL), lambda b, i: (b, 0, i, 0)), whole(1), whole(2)],
        out_specs=pl.BlockSpec((1, tq, D_MODEL), lambda b, i: (b, i, 0)),
        out_shape=jax.ShapeDtypeStruct((bsz, s, D_MODEL), BF16),
        scratch_shapes=[pltpu.VMEM((tq, D_MODEL), F32), pltpu.VMEM((PAIRS, 2 * tq, 1), F32)],
        compiler_params=_cparams("parallel", "arbitrary"),
        name="sb_attention",
    )(h, h, h)


def _band_kernel(*refs, max_dist, kv_quad, has_sink, with_lse):
    if has_sink:
        sink_ref, refs = refs[0], refs[1:]
    q_ref, kp_ref, kc_ref, vp_ref, vc_ref, o_ref = refs[:6]
    lse_ref = refs[6] if with_lse else None
    n = pl.program_id(2)
    row = lax.broadcasted_iota(jnp.int32, (QUAD * BLK, 2 * BLK), 0)
    kj = lax.broadcasted_iota(jnp.int32, (QUAD * BLK, 2 * BLK), 1)
    diff = BLK + row % BLK - kj
    valid = (diff >= 0) & (diff <= max_dist) & ((kj >= BLK) | (n > 0))
    lane = lax.broadcasted_iota(jnp.int32, (1, QUAD_W), 1)
    qk_sel = [(lane % LANES) // HALF_DIM == hq for hq in range(QUAD)]
    in_a = lax.broadcasted_iota(jnp.int32, (1, LANES), 1) < HEAD_DIM
    head_of_row = lax.broadcasted_iota(jnp.int32, (QUAD * BLK, 1), 0) // BLK
    for g in range(N_QUADS):
        q4 = q_ref[0, 0, :, g * QUAD_W:(g + 1) * QUAD_W]
        zero = jnp.zeros_like(q4)
        c0 = kv_quad(g) * QUAD_W
        kk = jnp.concatenate([kp_ref[0, 0, :, c0:c0 + QUAD_W], kc_ref[0, 0, :, c0:c0 + QUAD_W]], axis=0)
        qs = jnp.concatenate([jnp.where(qk_sel[hq], q4, zero) for hq in range(QUAD)], axis=0)
        s = jnp.where(valid, _dot_nt(qs, kk), NEG_INF)
        m = jnp.max(s, axis=1, keepdims=True)
        if has_sink:
            sink = jnp.full((QUAD * BLK, 1), sink_ref[QUAD * g], F32)
            for hq in range(1, QUAD):
                sink = jnp.where(head_of_row == hq, sink_ref[QUAD * g + hq], sink)
            m = jnp.maximum(m, sink)
        e = jnp.exp(s - m)
        den = jnp.sum(e, axis=1, keepdims=True)
        if has_sink:
            den = den + jnp.exp(sink - m)
        eb = e.astype(BF16)
        lse = m + jnp.log(den)
        for pr in range(QUAD // 2):
            c1 = c0 + pr * LANES
            rows = slice(2 * pr * BLK, 2 * (pr + 1) * BLK)
            vv = jnp.concatenate([vp_ref[0, 0, :, c1:c1 + LANES], vc_ref[0, 0, :, c1:c1 + LANES]], axis=0)
            o2 = _dot(eb[rows], vv) / den[rows]
            col = g * QUAD_W + pr * LANES
            o_ref[0, 0, :, col:col + LANES] = jnp.where(in_a, o2[:BLK], o2[BLK:]).astype(o_ref.dtype)
            if with_lse:
                l2 = lse[rows]
                lse_ref[0, 0, :, col:col + LANES] = jnp.where(in_a, l2[:BLK], l2[BLK:])


def _band_attention(h, max_dist, kv_width, kv_quad, sinks=None, with_lse=False, out_dtype=BF16):
    bsz, dil, ls, _ = h.shape
    k_blk = D_MODEL // kv_width
    cur = lambda blk: (lambda b, r, n: (b, r, n, blk))
    prev = lambda blk: (lambda b, r, n: (b, r, jnp.maximum(n - 1, 0), blk))
    kv_spec = lambda imap: pl.BlockSpec((1, 1, BLK, kv_width), imap)
    o_spec = pl.BlockSpec((1, 1, BLK, D_MODEL), lambda b, r, n: (b, r, n, 0))
    in_specs = [pl.BlockSpec((1, 1, BLK, D_MODEL), cur(0)), kv_spec(prev(k_blk)), kv_spec(cur(k_blk)),
                kv_spec(prev(k_blk + 1)), kv_spec(cur(k_blk + 1))]
    args = [h, h, h, h, h]
    if sinks is not None:
        in_specs = [pl.BlockSpec(memory_space=pltpu.SMEM)] + in_specs
        args = [sinks] + args
    out_shape = [jax.ShapeDtypeStruct((bsz, dil, ls, D_MODEL), out_dtype)]
    out_specs = [o_spec]
    if with_lse:
        out_shape.append(jax.ShapeDtypeStruct((bsz, dil, ls, D_MODEL), F32))
        out_specs.append(o_spec)
    res = pl.pallas_call(
        functools.partial(_band_kernel, max_dist=max_dist, kv_quad=kv_quad, has_sink=sinks is not None,
                          with_lse=with_lse),
        grid=(bsz, dil, ls // BLK),
        in_specs=in_specs,
        out_specs=out_specs,
        out_shape=out_shape,
        compiler_params=_cparams("parallel", "parallel", "arbitrary"),
        name="band_attention",
    )(*args)
    return res if with_lse else res[0]


def _out_ln_kernel(o_ref, w_ref, x_ref, g_ref, b_ref, y_ref, *, alpha):
    mix = _dot(o_ref[...], w_ref[...])
    y_ref[...] = _layernorm(alpha * x_ref[...] + mix, g_ref[...], b_ref[...])


def _out_ln(o, w, layer, x, g, b, alpha, tm=512):
    t = x.shape[0]
    row = pl.BlockSpec((tm, D_MODEL), lambda i: (i, 0))
    return pl.pallas_call(
        functools.partial(_out_ln_kernel, alpha=alpha),
        grid=(t // tm,),
        in_specs=[row, _layer_spec(w, layer), row, _const_spec(g.shape), _const_spec(b.shape)],
        out_specs=row,
        out_shape=jax.ShapeDtypeStruct((t, D_MODEL), F32),
        compiler_params=_cparams("parallel"),
        name="out_ln",
    )(o, w, x, g, b)


def _dil_out_ln_kernel(o0_ref, o1_ref, o2_ref, l0_ref, l1_ref, l2_ref, w_ref, x_ref, g_ref, b_ref, y_ref,
                       *, alpha):
    l0, l1, l2 = l0_ref[0], l1_ref[0, 0], l2_ref[0, 0]
    m = jnp.maximum(jnp.maximum(l0, l1), l2)
    e0, e1, e2 = jnp.exp(l0 - m), jnp.exp(l1 - m), jnp.exp(l2 - m)
    den = e0 + e1 + e2
    merged = (e0 * o0_ref[0] + e1 * o1_ref[0, 0] + e2 * o2_ref[0, 0]) / den
    mix = _dot(merged.astype(BF16), w_ref[...])
    y_ref[0] = _layernorm(alpha * x_ref[0] + mix, g_ref[...], b_ref[...])


def _dil_out_ln(outs, lses, w, layer, x, g, b, alpha, tm=256):
    bsz, s, d = x.shape
    big = DIL_GROUPS[-1][1]
    mid = DIL_GROUPS[1][1]
    ls = s // big
    tm = min(tm, ls)
    v0 = lambda a: a.reshape(bsz, ls, big * d)
    v1 = lambda a: a.reshape(bsz, mid, ls, (big // mid) * d)
    s0 = pl.BlockSpec((1, tm, d), lambda b, r, i: (b, i, r))
    s1 = pl.BlockSpec((1, 1, tm, d), lambda b, r, i: (b, r % mid, i, r // mid))
    s2 = pl.BlockSpec((1, 1, tm, d), lambda b, r, i: (b, r, i, 0))
    y = pl.pallas_call(
        functools.partial(_dil_out_ln_kernel, alpha=alpha),
        grid=(bsz, big, ls // tm),
        in_specs=[s0, s1, s2, s0, s1, s2, _layer_spec(w, layer), s0, _const_spec(g.shape), _const_spec(b.shape)],
        out_specs=s0,
        out_shape=jax.ShapeDtypeStruct((bsz, ls, big * d), F32),
        compiler_params=_cparams("parallel", "parallel", "parallel"),
        name="dil_out_ln",
    )(v0(outs[0]), v1(outs[1]), outs[2], v0(lses[0]), v1(lses[1]), lses[2], w, v0(x), g, b)
    return y.reshape(bsz, s, d)


def _quad_cols(w):
    d, n = w.shape
    w = w.reshape(d, n // QUAD_W, QUAD, 2, HALF_DIM).transpose(0, 1, 3, 2, 4)
    return w.reshape(d, n)


def _rope_tables(seq):
    pos = jnp.arange(seq, dtype=F32)
    inv = ROPE_THETA ** (-jnp.arange(0, HEAD_DIM, 2, dtype=F32) / HEAD_DIM)
    ang = pos[:, None] * jnp.tile(inv, LANES // HALF_DIM)[None, :]
    return jnp.cos(ang), jnp.sin(ang)


def kernel(x, p, ffn1_w_in, ffn1_w_out, ffn2_w_in, ffn2_w_out, ln_g, ln_b, sb_w_in, sb_w_out, swa_w_in,
           swa_sinks, swa_w_out, dil_w_in, dil_w_out, ple_w_proj, ple_w_gate):
    bsz, seq, d = x.shape
    depth = p.shape[0]
    t = bsz * seq
    alpha = (2 * depth) ** 0.25
    q_scale = HEAD_DIM ** -0.5
    width = HEADS * HEAD_DIM
    tables = _rope_tables(seq)
    ffn1_wi, ffn2_wi = _cast_bf16(ffn1_w_in), _cast_bf16(ffn2_w_in)
    ffn1_wo, ffn2_wo = _cast_bf16(ffn1_w_out, 0.5), _cast_bf16(ffn2_w_out, 0.5)
    ple_gate, ple_proj = _cast_bf16(ple_w_gate), _cast_bf16(ple_w_proj)
    sb_wo, swa_wo, dil_wo = _cast_bf16(sb_w_out), _cast_bf16(swa_w_out), _cast_bf16(dil_w_out)
    p = p.reshape(depth, t, PLE_DIM)
    x = x.reshape(t, d)
    for i in range(depth):
        kind, j = i % N_MIXERS, i // N_MIXERS
        ln = lambda k: (ln_g[i, k].reshape(1, d), ln_b[i, k].reshape(1, d))
        x = _ffn(x, i, ffn1_wi, ffn1_wo, *ln(0), alpha)
        x3d = x.reshape(bsz, seq, d)
        if kind == 0:
            w = sb_w_in[j]
            w = jnp.concatenate([w[:, :width] * q_scale, w[:, width:]], axis=1).astype(BF16)
            h = _proj(x3d, w, 1, (False,) * (w.shape[1] // QUAD_W))
            o = _sb_attention(h).reshape(t, width)
            x = _out_ln(o, sb_wo, j, x, *ln(1), alpha)
        elif kind == 1:
            w = swa_w_in[j]
            kvw = SWA_KV_HEADS * HEAD_DIM
            wk = w[:, width:width + kvw].reshape(d, SWA_KV_HEADS, 2, 1, HALF_DIM)
            wv = w[:, width + kvw:].reshape(d, SWA_KV_HEADS, 1, HEAD_DIM)
            w = jnp.concatenate([_quad_cols(w[:, :width] * q_scale),
                                 jnp.repeat(wk, QUAD, axis=3).reshape(d, SWA_KV_HEADS * QUAD_W),
                                 jnp.repeat(wv, QUAD, axis=2).reshape(d, SWA_KV_HEADS * QUAD_W)],
                                axis=1).astype(BF16)
            roped = (True,) * (N_QUADS + SWA_KV_HEADS) + (False,) * SWA_KV_HEADS
            h = _proj(x3d, w, 1, roped, tables)
            o = _band_attention(h, SWA_WINDOW - 1, SWA_KV_HEADS * QUAD_W, lambda g: g // (N_QUADS // SWA_KV_HEADS),
                                sinks=swa_sinks[j])
            x = _out_ln(o.reshape(t, width), swa_wo, j, x, *ln(1), alpha)
        else:
            outs, lses = [], []
            for gi, (win, dil) in enumerate(DIL_GROUPS):
                w = dil_w_in[j][:, gi * 3 * width:(gi + 1) * 3 * width]
                w = jnp.concatenate([_quad_cols(w[:, :width] * q_scale), _quad_cols(w[:, width:2 * width]),
                                     w[:, 2 * width:]], axis=1).astype(BF16)
                h = _proj(x3d, w, dil, (True,) * (2 * N_QUADS) + (False,) * N_QUADS, tables)
                o, lse = _band_attention(h, win // dil, width, lambda g: g, with_lse=True, out_dtype=F32)
                outs.append(o)
                lses.append(lse)
            x = _dil_out_ln(outs, lses, dil_wo, j, x3d, *ln(1), alpha).reshape(t, d)
        x = _ffn(x, i, ffn2_wi, ffn2_wo, *ln(2), alpha, ple=(p, ple_gate, ple_proj))
    return x.reshape(bsz, seq, d)
```

```python
import functools

import jax
import jax.numpy as jnp
from jax import lax
from jax.experimental import pallas as pl
from jax.experimental.pallas import tpu as pltpu

F32 = jnp.float32
BF16 = jnp.bfloat16

D_MODEL = 1024
D_FF = 2816
PLE_DIM = 256
LN_EPS = 1e-5
ROPE_THETA = 10000.0
N_MIXERS = 3
HEADS = 16
HEAD_DIM = 64
HALF_DIM = HEAD_DIM // 2
SWA_KV_HEADS = 2
SWA_WINDOW = 128
DIL_GROUPS = ((128, 1), (512, 4), (2048, 16))
BLK = 128
NEG_INF = -1e30
SB_DEAD_MASS = 110.0

LANES = 128
PAIRS = HEADS // 2
QUAD = 4
QUAD_W = QUAD * HEAD_DIM
N_QUADS = HEADS // QUAD
FF_CHUNK = 256
N_FF_CHUNKS = D_FF // FF_CHUNK
PROJ_CHUNK = 2 * QUAD_W
VMEM_LIMIT = 56 * 1024 * 1024


def _cparams(*sem):
    return pltpu.CompilerParams(dimension_semantics=sem, vmem_limit_bytes=VMEM_LIMIT)


def _dot(a, b):
    return jnp.dot(a, b, preferred_element_type=F32)


def _dot_nt(a, b):
    return lax.dot_general(a, b, (((1,), (1,)), ((), ())), preferred_element_type=F32)


def _layernorm(y, g, b):
    mu = jnp.mean(y, axis=-1, keepdims=True)
    yc = y - mu
    var = jnp.mean(yc * yc, axis=-1, keepdims=True)
    return yc * lax.rsqrt(var + LN_EPS) * g + b


def _sigmoid(x):
    return 1.0 / (1.0 + jnp.exp(-x))


def _const_spec(shape):
    nd = len(shape)
    return pl.BlockSpec(shape, lambda *_: (0,) * nd, pipeline_mode=pl.Buffered(1))


def _layer_spec(stack, layer):
    return pl.BlockSpec((None,) + stack.shape[1:], lambda *_: (layer, 0, 0), pipeline_mode=pl.Buffered(1))


def _cast_kernel(w_ref, o_ref, *, scale):
    w = w_ref[...]
    if scale != 1.0:
        w = w * scale
    o_ref[...] = w.astype(o_ref.dtype)


def _cast_bf16(w, scale=1.0, rows=256):
    n_layers, r, c = w.shape
    rb = min(rows, r)
    spec = pl.BlockSpec((1, rb, c), lambda a, b: (a, b, 0))
    return pl.pallas_call(
        functools.partial(_cast_kernel, scale=scale),
        grid=(n_layers, r // rb),
        in_specs=[spec],
        out_specs=spec,
        out_shape=jax.ShapeDtypeStruct(w.shape, BF16),
        compiler_params=_cparams("parallel", "parallel"),
        name="cast_bf16",
    )(w)


def _ffn_body(x_ref, wi_ref, wo_ref, g_ref, b_ref, h_ref, alpha):
    xb = x_ref[...].astype(BF16)
    for c in range(N_FF_CHUNKS):
        lo = c * FF_CHUNK
        gate = _dot(xb, wi_ref[:, lo:lo + FF_CHUNK])
        up = _dot(xb, wi_ref[:, D_FF + lo:D_FF + lo + FF_CHUNK])
        h_ref[:, lo:lo + FF_CHUNK] = ((gate * _sigmoid(gate)) * up).astype(BF16)
    y = alpha * x_ref[...] + _dot(h_ref[...], wo_ref[...])
    return _layernorm(y, g_ref[...], b_ref[...])


def _ffn_kernel(x_ref, wi_ref, wo_ref, g_ref, b_ref, o_ref, h_ref, *, alpha):
    o_ref[...] = _ffn_body(x_ref, wi_ref, wo_ref, g_ref, b_ref, h_ref, alpha)


def _ffn_ple_kernel(x_ref, wi_ref, wo_ref, g_ref, b_ref, p_ref, pg_ref, pp_ref, o_ref, h_ref, *, alpha):
    x3 = _ffn_body(x_ref, wi_ref, wo_ref, g_ref, b_ref, h_ref, alpha)
    gate = _dot(x3.astype(BF16), pg_ref[...])
    proj = _dot(p_ref[...].astype(BF16), pp_ref[...])
    o_ref[...] = x3 + _sigmoid(gate) * proj


def _ffn(x, layer, wi, wo, ln, alpha, ple=None, tm=512):
    t = x.shape[0]
    row = pl.BlockSpec((tm, D_MODEL), lambda i: (i, 0))
    g, b, ln_row = ln
    in_specs = [row, _layer_spec(wi, layer), _layer_spec(wo, layer), _layer_spec(g, ln_row), _layer_spec(b, ln_row)]
    args = [x, wi, wo, g, b]
    if ple is None:
        body = functools.partial(_ffn_kernel, alpha=alpha)
    else:
        p, pg, pp = ple
        body = functools.partial(_ffn_ple_kernel, alpha=alpha)
        in_specs += [pl.BlockSpec((None, tm, PLE_DIM), lambda i: (layer, i, 0)), _layer_spec(pg, layer),
                     _layer_spec(pp, layer)]
        args += [p, pg, pp]
    return pl.pallas_call(
        body,
        grid=(t // tm,),
        in_specs=in_specs,
        out_specs=row,
        out_shape=jax.ShapeDtypeStruct((t, D_MODEL), F32),
        scratch_shapes=[pltpu.VMEM((tm, D_FF), BF16)],
        compiler_params=_cparams("parallel"),
        name="ffn_ple" if ple is not None else "ffn",
    )(*args)


def _proj_kernel(*refs, roped, dil):
    refs = list(refs)
    slab_ref = refs.pop() if dil > 1 else None
    if any(roped):
        x_ref, w_ref, cos_ref, sin_ref, o_ref = refs
    else:
        x_ref, w_ref, o_ref = refs
    tm = x_ref.shape[1]
    rows = tm // dil

    def by_class(ref128):
        return jnp.concatenate([ref128[pl.ds(r, rows, stride=dil), :] for r in range(dil)], axis=0)

    if dil > 1:
        n_slabs = x_ref.shape[2] // LANES
        for s in range(n_slabs):
            slab_ref[s] = x_ref[0, :, s * LANES:(s + 1) * LANES]
        x = jnp.concatenate([by_class(slab_ref.at[s]) for s in range(n_slabs)], axis=1)
    else:
        x = x_ref[0]
    if any(roped):
        cos, sin = (by_class(cos_ref), by_class(sin_ref)) if dil > 1 else (cos_ref[...], sin_ref[...])

    def store(col, val):
        for r in range(dil):
            o_ref[0, r, :, col:col + val.shape[1]] = val[r * rows:(r + 1) * rows].astype(o_ref.dtype)

    xb = x.astype(BF16)
    per_chunk = PROJ_CHUNK // QUAD_W
    for c in range(len(roped) // per_chunk):
        acc = _dot(xb, w_ref[:, c * PROJ_CHUNK:(c + 1) * PROJ_CHUNK])
        for k in range(per_chunk):
            lo = k * QUAD_W
            col = c * PROJ_CHUNK + lo
            if roped[c * per_chunk + k]:
                x1, x2 = acc[:, lo:lo + LANES], acc[:, lo + LANES:lo + QUAD_W]
                store(col, x1 * cos - x2 * sin)
                store(col + LANES, x2 * cos + x1 * sin)
            else:
                store(col, acc[:, lo:lo + QUAD_W])


def _proj(x, w, dil, roped, tables=None, tm=512):
    bsz, s, d = x.shape
    n = w.shape[1]
    in_specs = [pl.BlockSpec((1, tm, d), lambda b, i: (b, i, 0)), _const_spec(w.shape)]
    args = [x, w]
    if any(roped):
        tab_spec = pl.BlockSpec((tm, LANES), lambda b, i: (i, 0))
        in_specs += [tab_spec, tab_spec]
        args += list(tables)
    return pl.pallas_call(
        functools.partial(_proj_kernel, roped=tuple(roped), dil=dil),
        grid=(bsz, s // tm),
        in_specs=in_specs,
        out_specs=pl.BlockSpec((1, dil, tm // dil, n), lambda b, i: (b, 0, i, 0)),
        out_shape=jax.ShapeDtypeStruct((bsz, dil, s // dil, n), BF16),
        scratch_shapes=[pltpu.VMEM((d // LANES, tm, LANES), F32)] if dil > 1 else [],
        compiler_params=_cparams("parallel", "parallel"),
        name="mixer_proj",
    )(*args)


def _sb_kernel(q_ref, k_ref, v_ref, o_ref, acc_ref, r_ref, *, tq):
    i = pl.program_id(1)
    has_prev = i > 0
    d_start = pl.multiple_of(i * tq, tq)
    p_start = pl.multiple_of(jnp.maximum(i - 1, 0) * tq, tq)
    in_a = lax.broadcasted_iota(jnp.int32, (1, LANES), 1) < HEAD_DIM
    row = lax.broadcasted_iota(jnp.int32, (tq, tq), 0)
    col = lax.broadcasted_iota(jnp.int32, (tq, tq), 1)
    suffix = (row >= col).astype(BF16)
    row2 = lax.broadcasted_iota(jnp.int32, (2 * tq, tq), 0) % tq
    col2 = lax.broadcasted_iota(jnp.int32, (2 * tq, tq), 1)
    strict = col2 < row2

    def stacked_q(sl):
        q2 = q_ref[0, 0, :, sl]
        zero = jnp.zeros_like(q2)
        return jnp.concatenate([jnp.where(in_a, q2, zero), jnp.where(in_a, zero, q2)], axis=0)

    def softplus(z):
        return jnp.maximum(z, 0.0) + jnp.log(1.0 + jnp.exp(-jnp.abs(z)))

    def fold(pv):
        return jnp.where(in_a, pv[:tq], pv[tq:])

    for p in range(PAIRS):
        sl = slice(p * LANES, (p + 1) * LANES)
        qs = stacked_q(sl)
        z = _dot_nt(qs, k_ref[0, 0, pl.ds(d_start, tq), sl])
        sp = jnp.where(strict, softplus(z), 0.0)
        a = jnp.where(strict, jnp.exp(z - _dot(sp.astype(BF16), suffix)), 0.0)
        pv = _dot(a.astype(BF16), v_ref[0, 0, pl.ds(d_start, tq), sl])
        r = jnp.sum(sp, axis=1, keepdims=True)
        z = _dot_nt(qs, k_ref[0, 0, pl.ds(p_start, tq), sl])
        sp = softplus(z)
        a = jnp.where(has_prev, jnp.exp(z - _dot(sp.astype(BF16), suffix) - r), 0.0)
        pv = pv + _dot(a.astype(BF16), v_ref[0, 0, pl.ds(p_start, tq), sl])
        acc_ref[:, sl] = fold(pv)
        r_ref[p] = r + jnp.where(has_prev, jnp.sum(sp, axis=1, keepdims=True), 0.0)

    def more(state):
        j, mass = state[0], state[1]
        return jnp.logical_and(j >= 0, mass < SB_DEAD_MASS)

    for p in range(PAIRS):
        sl = slice(p * LANES, (p + 1) * LANES)

        def step(state, sl=sl):
            j, _, acc, r = state
            start = pl.multiple_of(j * tq, tq)
            z = _dot_nt(stacked_q(sl), k_ref[0, 0, pl.ds(start, tq), sl])
            sp = softplus(z)
            a = jnp.exp(z - _dot(sp.astype(BF16), suffix) - r)
            acc = acc + fold(_dot(a.astype(BF16), v_ref[0, 0, pl.ds(start, tq), sl]))
            r = r + jnp.sum(sp, axis=1, keepdims=True)
            return (j - 1, jnp.min(r), acc, r)

        r = r_ref[p]
        state = lax.while_loop(more, step, (i - 2, jnp.min(r), acc_ref[:, sl], r))
        o_ref[0, :, sl] = state[2].astype(o_ref.dtype)


def _sb_attention(h, tq=256):
    bsz, _, s, _ = h.shape
    whole = lambda blk: pl.BlockSpec((1, 1, s, D_MODEL), lambda b, i: (b, 0, 0, blk), pipeline_mode=pl.Buffered(1))
    return pl.pallas_call(
        functools.partial(_sb_kernel, tq=tq),
        grid=(bsz, s // tq),
        in_specs=[pl.BlockSpec((1, 1, tq, D_MODEL), lambda b, i: (b, 0, i, 0)), whole(1), whole(2)],
        out_specs=pl.BlockSpec((1, tq, D_MODEL), lambda b, i: (b, i, 0)),
        out_shape=jax.ShapeDtypeStruct((bsz, s, D_MODEL), BF16),
        scratch_shapes=[pltpu.VMEM((tq, D_MODEL), F32), pltpu.VMEM((PAIRS, 2 * tq, 1), F32)],
        compiler_params=_cparams("parallel", "arbitrary"),
        name="sb_attention",
    )(h, h, h)


def _band_masks(n, max_dist):
    row = lax.broadcasted_iota(jnp.int32, (QUAD * BLK, 2 * BLK), 0)
    kj = lax.broadcasted_iota(jnp.int32, (QUAD * BLK, 2 * BLK), 1)
    diff = BLK + row % BLK - kj
    valid = (diff >= 0) & (diff <= max_dist) & ((kj >= BLK) | (n > 0))
    lane = lax.broadcasted_iota(jnp.int32, (1, QUAD_W), 1)
    qk_sel = [(lane % LANES) // HALF_DIM == hq for hq in range(QUAD)]
    in_a = lax.broadcasted_iota(jnp.int32, (1, LANES), 1) < HEAD_DIM
    return valid, qk_sel, in_a


def _band_quad(q4, kk, valid, qk_sel, sink=None):
    zero = jnp.zeros_like(q4)
    qs = jnp.concatenate([jnp.where(qk_sel[hq], q4, zero) for hq in range(QUAD)], axis=0)
    s = jnp.where(valid, _dot_nt(qs, kk), NEG_INF)
    m = jnp.max(s, axis=1, keepdims=True)
    if sink is not None:
        m = jnp.maximum(m, sink)
    e = jnp.exp(s - m)
    den = jnp.sum(e, axis=1, keepdims=True)
    if sink is not None:
        den = den + jnp.exp(sink - m)
    return e.astype(BF16), den, m + jnp.log(den)


def _band_kernel(sink_ref, q_ref, kp_ref, kc_ref, vp_ref, vc_ref, o_ref, *, max_dist, kv_quad):
    valid, qk_sel, in_a = _band_masks(pl.program_id(1), max_dist)
    head_of_row = lax.broadcasted_iota(jnp.int32, (QUAD * BLK, 1), 0) // BLK
    for g in range(N_QUADS):
        c0 = kv_quad(g) * QUAD_W
        kk = jnp.concatenate([kp_ref[0, 0, :, c0:c0 + QUAD_W], kc_ref[0, 0, :, c0:c0 + QUAD_W]], axis=0)
        sink = jnp.full((QUAD * BLK, 1), sink_ref[QUAD * g], F32)
        for hq in range(1, QUAD):
            sink = jnp.where(head_of_row == hq, sink_ref[QUAD * g + hq], sink)
        eb, den, _ = _band_quad(q_ref[0, 0, :, g * QUAD_W:(g + 1) * QUAD_W], kk, valid, qk_sel, sink)
        for pr in range(QUAD // 2):
            c1 = c0 + pr * LANES
            rows = slice(2 * pr * BLK, 2 * (pr + 1) * BLK)
            vv = jnp.concatenate([vp_ref[0, 0, :, c1:c1 + LANES], vc_ref[0, 0, :, c1:c1 + LANES]], axis=0)
            o2 = _dot(eb[rows], vv) / den[rows]
            col = g * QUAD_W + pr * LANES
            o_ref[0, :, col:col + LANES] = jnp.where(in_a, o2[:BLK], o2[BLK:]).astype(o_ref.dtype)


def _band_attention(h, max_dist, kv_width, kv_quad, sinks):
    bsz, _, s, _ = h.shape
    k_blk = D_MODEL // kv_width
    cur = lambda blk: (lambda b, n: (b, 0, n, blk))
    prev = lambda blk: (lambda b, n: (b, 0, jnp.maximum(n - 1, 0), blk))
    kv_spec = lambda imap: pl.BlockSpec((1, 1, BLK, kv_width), imap)
    return pl.pallas_call(
        functools.partial(_band_kernel, max_dist=max_dist, kv_quad=kv_quad),
        grid=(bsz, s // BLK),
        in_specs=[pl.BlockSpec(memory_space=pltpu.SMEM), pl.BlockSpec((1, 1, BLK, D_MODEL), cur(0)),
                  kv_spec(prev(k_blk)), kv_spec(cur(k_blk)), kv_spec(prev(k_blk + 1)), kv_spec(cur(k_blk + 1))],
        out_specs=pl.BlockSpec((1, BLK, D_MODEL), lambda b, n: (b, n, 0)),
        out_shape=jax.ShapeDtypeStruct((bsz, s, D_MODEL), BF16),
        compiler_params=_cparams("parallel", "arbitrary"),
        name="band_attention",
    )(sinks, h, h, h, h, h)


def _band_dil_kernel(q_ref, kp_ref, kc_ref, vp_ref, vc_ref, o_ref, lse_ref, *, max_dist, dil, quads):
    valid, qk_sel, in_a = _band_masks(pl.program_id(1), max_dist)

    def one_class(r):
        dst = pl.ds(r, BLK, stride=dil) if dil > 1 else slice(None)
        for g in range(quads):
            cols = slice(g * QUAD_W, (g + 1) * QUAD_W)
            kk = jnp.concatenate([kp_ref[0, r, :, cols], kc_ref[0, r, :, cols]], axis=0)
            eb, den, lse = _band_quad(q_ref[0, r, :, cols], kk, valid, qk_sel)
            for pr in range(QUAD // 2):
                c1 = slice(g * QUAD_W + pr * LANES, g * QUAD_W + (pr + 1) * LANES)
                rows = slice(2 * pr * BLK, 2 * (pr + 1) * BLK)
                vv = jnp.concatenate([vp_ref[0, r, :, c1], vc_ref[0, r, :, c1]], axis=0)
                o2 = _dot(eb[rows], vv) / den[rows]
                l2 = lse[rows]
                o_ref[0, 2 * g + pr, dst, :] = jnp.where(in_a, o2[:BLK], o2[BLK:])
                lse_ref[0, 2 * g + pr, dst, :] = jnp.where(in_a, l2[:BLK], l2[BLK:])

    if dil == 1:
        one_class(0)
    else:
        def body(r, carry):
            one_class(r)
            return carry
        lax.fori_loop(0, dil, body, 0)


def _band_dil_attention(h, max_dist):
    bsz, dil, ls, _ = h.shape
    quads = N_QUADS if dil <= QUAD else N_QUADS // 2
    width = quads * QUAD_W
    nq = D_MODEL // width
    cur = lambda base: (lambda b, n, g: (b, 0, n, base + g))
    prev = lambda base: (lambda b, n, g: (b, 0, jnp.maximum(n - 1, 0), base + g))
    spec = lambda imap: pl.BlockSpec((1, dil, BLK, width), imap)
    o_spec = pl.BlockSpec((1, 2 * quads, BLK * dil, LANES), lambda b, n, g: (b, g, n, 0))
    o_shape = jax.ShapeDtypeStruct((bsz, PAIRS, ls * dil, LANES), F32)
    return pl.pallas_call(
        functools.partial(_band_dil_kernel, max_dist=max_dist, dil=dil, quads=quads),
        grid=(bsz, ls // BLK, nq),
        in_specs=[spec(cur(0)), spec(prev(nq)), spec(cur(nq)), spec(prev(2 * nq)), spec(cur(2 * nq))],
        out_specs=[o_spec, o_spec],
        out_shape=[o_shape, o_shape],
        compiler_params=_cparams("parallel", "arbitrary", "arbitrary"),
        name="band_dil_attention",
    )(h, h, h, h, h)


def _out_ln_kernel(o_ref, w_ref, x_ref, g_ref, b_ref, y_ref, *, alpha):
    mix = _dot(o_ref[...], w_ref[...])
    y_ref[...] = _layernorm(alpha * x_ref[...] + mix, g_ref[...], b_ref[...])


def _out_ln(o, w, layer, x, ln, alpha, tm=512):
    t = x.shape[0]
    g, b, ln_row = ln
    row = pl.BlockSpec((tm, D_MODEL), lambda i: (i, 0))
    return pl.pallas_call(
        functools.partial(_out_ln_kernel, alpha=alpha),
        grid=(t // tm,),
        in_specs=[row, _layer_spec(w, layer), row, _layer_spec(g, ln_row), _layer_spec(b, ln_row)],
        out_specs=row,
        out_shape=jax.ShapeDtypeStruct((t, D_MODEL), F32),
        compiler_params=_cparams("parallel"),
        name="out_ln",
    )(o, w, x, g, b)


def _dil_out_ln_kernel(o0_ref, o1_ref, o2_ref, l0_ref, l1_ref, l2_ref, w_ref, x_ref, g_ref, b_ref, y_ref,
                       *, alpha):
    parts = []
    for p in range(PAIRS):
        l0, l1, l2 = l0_ref[0, p], l1_ref[0, p], l2_ref[0, p]
        m = jnp.maximum(jnp.maximum(l0, l1), l2)
        e0, e1, e2 = jnp.exp(l0 - m), jnp.exp(l1 - m), jnp.exp(l2 - m)
        merged = (e0 * o0_ref[0, p] + e1 * o1_ref[0, p] + e2 * o2_ref[0, p]) / (e0 + e1 + e2)
        parts.append(merged.astype(BF16))
    mix = _dot(jnp.concatenate(parts, axis=1), w_ref[...])
    y_ref[0] = _layernorm(alpha * x_ref[0] + mix, g_ref[...], b_ref[...])


def _dil_out_ln(outs, lses, w, layer, x, ln, alpha, tm=256):
    bsz, s, d = x.shape
    g, b, ln_row = ln
    slab = pl.BlockSpec((1, PAIRS, tm, LANES), lambda bi, i: (bi, 0, i, 0))
    row = pl.BlockSpec((1, tm, d), lambda bi, i: (bi, i, 0))
    return pl.pallas_call(
        functools.partial(_dil_out_ln_kernel, alpha=alpha),
        grid=(bsz, s // tm),
        in_specs=[slab] * 6 + [_layer_spec(w, layer), row, _layer_spec(g, ln_row), _layer_spec(b, ln_row)],
        out_specs=row,
        out_shape=jax.ShapeDtypeStruct((bsz, s, d), F32),
        compiler_params=_cparams("parallel", "parallel"),
        name="dil_out_ln",
    )(*outs, *lses, w, x, g, b)


def _quad_cols(w):
    d, n = w.shape
    w = w.reshape(d, n // QUAD_W, QUAD, 2, HALF_DIM).transpose(0, 1, 3, 2, 4)
    return w.reshape(d, n)


def _rope_tables(seq):
    pos = jnp.arange(seq, dtype=F32)
    inv = ROPE_THETA ** (-jnp.arange(0, HEAD_DIM, 2, dtype=F32) / HEAD_DIM)
    ang = pos[:, None] * jnp.tile(inv, LANES // HALF_DIM)[None, :]
    return jnp.cos(ang), jnp.sin(ang)


def kernel(x, p, ffn1_w_in, ffn1_w_out, ffn2_w_in, ffn2_w_out, ln_g, ln_b, sb_w_in, sb_w_out, swa_w_in,
           swa_sinks, swa_w_out, dil_w_in, dil_w_out, ple_w_proj, ple_w_gate):
    bsz, seq, d = x.shape
    depth = p.shape[0]
    t = bsz * seq
    alpha = (2 * depth) ** 0.25
    q_scale = HEAD_DIM ** -0.5
    width = HEADS * HEAD_DIM
    tables = _rope_tables(seq)
    ffn1_wi, ffn2_wi = _cast_bf16(ffn1_w_in), _cast_bf16(ffn2_w_in)
    ffn1_wo, ffn2_wo = _cast_bf16(ffn1_w_out, 0.5), _cast_bf16(ffn2_w_out, 0.5)
    ple_gate, ple_proj = _cast_bf16(ple_w_gate), _cast_bf16(ple_w_proj)
    sb_wo, swa_wo, dil_wo = _cast_bf16(sb_w_out), _cast_bf16(swa_w_out), _cast_bf16(dil_w_out)
    p = p.reshape(depth, t, PLE_DIM)
    n_ln = ln_g.shape[1]
    ln_g = ln_g.reshape(depth * n_ln, 1, d)
    ln_b = ln_b.reshape(depth * n_ln, 1, d)
    x = x.reshape(t, d)
    for i in range(depth):
        kind, j = i % N_MIXERS, i // N_MIXERS
        ln = lambda k: (ln_g, ln_b, i * n_ln + k)
        x = _ffn(x, i, ffn1_wi, ffn1_wo, ln(0), alpha)
        x3d = x.reshape(bsz, seq, d)
        if kind == 0:
            w = sb_w_in[j]
            w = jnp.concatenate([w[:, :width] * q_scale, w[:, width:]], axis=1).astype(BF16)
            h = _proj(x3d, w, 1, (False,) * (w.shape[1] // QUAD_W))
            o = _sb_attention(h).reshape(t, width)
            x = _out_ln(o, sb_wo, j, x, ln(1), alpha)
        elif kind == 1:
            w = swa_w_in[j]
            kvw = SWA_KV_HEADS * HEAD_DIM
            wk = w[:, width:width + kvw].reshape(d, SWA_KV_HEADS, 2, 1, HALF_DIM)
            wv = w[:, width + kvw:].reshape(d, SWA_KV_HEADS, 1, HEAD_DIM)
            w = jnp.concatenate([_quad_cols(w[:, :width] * q_scale),
                                 jnp.repeat(wk, QUAD, axis=3).reshape(d, SWA_KV_HEADS * QUAD_W),
                                 jnp.repeat(wv, QUAD, axis=2).reshape(d, SWA_KV_HEADS * QUAD_W)],
                                axis=1).astype(BF16)
            roped = (True,) * (N_QUADS + SWA_KV_HEADS) + (False,) * SWA_KV_HEADS
            h = _proj(x3d, w, 1, roped, tables)
            o = _band_attention(h, SWA_WINDOW - 1, SWA_KV_HEADS * QUAD_W, lambda g: g // (N_QUADS // SWA_KV_HEADS),
                                swa_sinks[j])
            x = _out_ln(o.reshape(t, width), swa_wo, j, x, ln(1), alpha)
        else:
            outs, lses = [], []
            for gi, (win, dil) in enumerate(DIL_GROUPS):
                w = dil_w_in[j][:, gi * 3 * width:(gi + 1) * 3 * width]
                w = jnp.concatenate([_quad_cols(w[:, :width] * q_scale), _quad_cols(w[:, width:2 * width]),
                                     w[:, 2 * width:]], axis=1).astype(BF16)
                h = _proj(x3d, w, dil, (True,) * (2 * N_QUADS) + (False,) * N_QUADS, tables)
                o, lse = _band_dil_attention(h, win // dil)
                outs.append(o)
                lses.append(lse)
            x = _dil_out_ln(outs, lses, dil_wo, j, x3d, ln(1), alpha).reshape(t, d)
        x = _ffn(x, i, ffn2_wi, ffn2_wo, ln(2), alpha, ple=(p, ple_gate, ple_proj))
    return x.reshape(bsz, seq, d)
```

```python
import functools

import jax
import jax.numpy as jnp
from jax import lax
from jax.experimental import pallas as pl
from jax.experimental.pallas import tpu as pltpu

F32 = jnp.float32
BF16 = jnp.bfloat16

D_MODEL = 1024
D_FF = 2816
PLE_DIM = 256
LN_EPS = 1e-5
ROPE_THETA = 10000.0
N_MIXERS = 3
HEADS = 16
HEAD_DIM = 64
HALF_DIM = HEAD_DIM // 2
SWA_KV_HEADS = 2
SWA_WINDOW = 128
DIL_GROUPS = ((128, 1), (512, 4), (2048, 16))
BLK = 128
NEG_INF = -1e30
LOG2E = 1.4426950408889634
LN2 = 0.6931471805599453
SB_DEAD_MASS = 110.0 * LOG2E

LANES = 128
PAIRS = HEADS // 2
QUAD = 4
QUAD_W = QUAD * HEAD_DIM
N_QUADS = HEADS // QUAD
FF_CHUNK = 256
N_FF_CHUNKS = D_FF // FF_CHUNK
PROJ_CHUNK = 2 * QUAD_W
VMEM_LIMIT = 56 * 1024 * 1024


def _cparams(*sem):
    return pltpu.CompilerParams(dimension_semantics=sem, vmem_limit_bytes=VMEM_LIMIT)


def _dot(a, b):
    return jnp.dot(a, b, preferred_element_type=F32)


def _dot_nt(a, b):
    return lax.dot_general(a, b, (((1,), (1,)), ((), ())), preferred_element_type=F32)


def _layernorm(y, g, b):
    mu = jnp.mean(y, axis=-1, keepdims=True)
    yc = y - mu
    var = jnp.mean(yc * yc, axis=-1, keepdims=True)
    return yc * lax.rsqrt(var + LN_EPS) * g + b


def _sigmoid(x):
    return 1.0 / (1.0 + jnp.exp(-x))


def _const_spec(shape):
    nd = len(shape)
    return pl.BlockSpec(shape, lambda *_: (0,) * nd, pipeline_mode=pl.Buffered(1))


def _layer_spec(stack, layer):
    return pl.BlockSpec((None,) + stack.shape[1:], lambda *_: (layer, 0, 0), pipeline_mode=pl.Buffered(1))


def _cast_kernel(w_ref, o_ref, *, scale):
    w = w_ref[...]
    if scale != 1.0:
        w = w * scale
    o_ref[...] = w.astype(o_ref.dtype)


def _cast_bf16(w, scale=1.0, rows=256):
    n_layers, r, c = w.shape
    rb = min(rows, r)
    spec = pl.BlockSpec((1, rb, c), lambda a, b: (a, b, 0))
    return pl.pallas_call(
        functools.partial(_cast_kernel, scale=scale),
        grid=(n_layers, r // rb),
        in_specs=[spec],
        out_specs=spec,
        out_shape=jax.ShapeDtypeStruct(w.shape, BF16),
        compiler_params=_cparams("parallel", "parallel"),
        name="cast_bf16",
    )(w)


def _ffn_body(x_ref, wi_ref, wo_ref, g_ref, b_ref, h_ref, alpha):
    xb = x_ref[...].astype(BF16)
    for c in range(N_FF_CHUNKS):
        lo = c * FF_CHUNK
        gate = _dot(xb, wi_ref[:, lo:lo + FF_CHUNK])
        up = _dot(xb, wi_ref[:, D_FF + lo:D_FF + lo + FF_CHUNK])
        h_ref[:, lo:lo + FF_CHUNK] = ((gate * _sigmoid(gate)) * up).astype(BF16)
    y = alpha * x_ref[...] + _dot(h_ref[...], wo_ref[...])
    return _layernorm(y, g_ref[...], b_ref[...])


def _ffn_kernel(x_ref, wi_ref, wo_ref, g_ref, b_ref, o_ref, h_ref, *, alpha):
    o_ref[...] = _ffn_body(x_ref, wi_ref, wo_ref, g_ref, b_ref, h_ref, alpha)


def _ffn_ple_kernel(x_ref, wi_ref, wo_ref, g_ref, b_ref, p_ref, pg_ref, pp_ref, o_ref, h_ref, *, alpha):
    x3 = _ffn_body(x_ref, wi_ref, wo_ref, g_ref, b_ref, h_ref, alpha)
    gate = _dot(x3.astype(BF16), pg_ref[...])
    proj = _dot(p_ref[...].astype(BF16), pp_ref[...])
    o_ref[...] = x3 + _sigmoid(gate) * proj


def _mix_ffn_ple_kernel(a_ref, wm_ref, gm_ref, bm_ref, x_ref, wi_ref, wo_ref, g_ref, b_ref, p_ref, pg_ref, pp_ref,
                        o_ref, h_ref, x2_ref, *, alpha):
    x2_ref[...] = _layernorm(alpha * x_ref[...] + _dot(a_ref[...], wm_ref[...]), gm_ref[...], bm_ref[...])
    _ffn_ple_kernel(x2_ref, wi_ref, wo_ref, g_ref, b_ref, p_ref, pg_ref, pp_ref, o_ref, h_ref, alpha=alpha)


def _ffn(x, layer, wi, wo, ln, alpha, ple=None, mix=None, tm=512):
    t = x.shape[0]
    row = pl.BlockSpec((tm, D_MODEL), lambda i: (i, 0))
    g, b, ln_row = ln
    in_specs = [row, _layer_spec(wi, layer), _layer_spec(wo, layer), _layer_spec(g, ln_row), _layer_spec(b, ln_row)]
    args = [x, wi, wo, g, b]
    scratch = [pltpu.VMEM((tm, D_FF), BF16)]
    if ple is None:
        body = functools.partial(_ffn_kernel, alpha=alpha)
    else:
        p, pg, pp = ple
        body = functools.partial(_ffn_ple_kernel, alpha=alpha)
        in_specs += [pl.BlockSpec((None, tm, PLE_DIM), lambda i: (layer, i, 0)), _layer_spec(pg, layer),
                     _layer_spec(pp, layer)]
        args += [p, pg, pp]
    if mix is not None:
        a, wm, mix_layer, (gm, bm, mix_row) = mix
        body = functools.partial(_mix_ffn_ple_kernel, alpha=alpha)
        in_specs = [row, _layer_spec(wm, mix_layer), _layer_spec(gm, mix_row), _layer_spec(bm, mix_row)] + in_specs
        args = [a, wm, gm, bm] + args
        scratch.append(pltpu.VMEM((tm, D_MODEL), F32))
    return pl.pallas_call(
        body,
        grid=(t // tm,),
        in_specs=in_specs,
        out_specs=row,
        out_shape=jax.ShapeDtypeStruct((t, D_MODEL), F32),
        scratch_shapes=scratch,
        compiler_params=_cparams("parallel"),
        name=("mix_ffn_ple" if mix is not None else "ffn_ple") if ple is not None else "ffn",
    )(*args)


def _proj_kernel(*refs, roped, dil):
    refs = list(refs)
    slab_ref = refs.pop() if dil > 1 else None
    if any(roped):
        x_ref, w_ref, cos_ref, sin_ref, o_ref = refs
    else:
        x_ref, w_ref, o_ref = refs
    tm = x_ref.shape[1]
    rows = tm // dil

    def by_class(ref128):
        return jnp.concatenate([ref128[pl.ds(r, rows, stride=dil), :] for r in range(dil)], axis=0)

    if dil > 1:
        n_slabs = x_ref.shape[2] // LANES
        for s in range(n_slabs):
            slab_ref[s] = x_ref[0, :, s * LANES:(s + 1) * LANES]
        x = jnp.concatenate([by_class(slab_ref.at[s]) for s in range(n_slabs)], axis=1)
    else:
        x = x_ref[0]
    if any(roped):
        cos, sin = (by_class(cos_ref), by_class(sin_ref)) if dil > 1 else (cos_ref[...], sin_ref[...])

    def store(col, val):
        for r in range(dil):
            o_ref[0, r, :, col:col + val.shape[1]] = val[r * rows:(r + 1) * rows].astype(o_ref.dtype)

    xb = x.astype(BF16)
    per_chunk = PROJ_CHUNK // QUAD_W
    for c in range(len(roped) // per_chunk):
        acc = _dot(xb, w_ref[:, c * PROJ_CHUNK:(c + 1) * PROJ_CHUNK])
        for k in range(per_chunk):
            lo = k * QUAD_W
            col = c * PROJ_CHUNK + lo
            if roped[c * per_chunk + k]:
                x1, x2 = acc[:, lo:lo + LANES], acc[:, lo + LANES:lo + QUAD_W]
                store(col, x1 * cos - x2 * sin)
                store(col + LANES, x2 * cos + x1 * sin)
            else:
                store(col, acc[:, lo:lo + QUAD_W])


def _proj(x, w, dil, roped, tables=None, tm=512):
    bsz, s, d = x.shape
    n = w.shape[1]
    in_specs = [pl.BlockSpec((1, tm, d), lambda b, i: (b, i, 0)), _const_spec(w.shape)]
    args = [x, w]
    if any(roped):
        tab_spec = pl.BlockSpec((tm, LANES), lambda b, i: (i, 0))
        in_specs += [tab_spec, tab_spec]
        args += list(tables)
    return pl.pallas_call(
        functools.partial(_proj_kernel, roped=tuple(roped), dil=dil),
        grid=(bsz, s // tm),
        in_specs=in_specs,
        out_specs=pl.BlockSpec((1, dil, tm // dil, n), lambda b, i: (b, 0, i, 0)),
        out_shape=jax.ShapeDtypeStruct((bsz, dil, s // dil, n), BF16),
        scratch_shapes=[pltpu.VMEM((d // LANES, tm, LANES), F32)] if dil > 1 else [],
        compiler_params=_cparams("parallel", "parallel"),
        name="mixer_proj",
    )(*args)


def _sb_kernel(q_ref, k_ref, v_ref, o_ref, acc_ref, r_ref, *, tq):
    i = pl.program_id(1)
    has_prev = i > 0
    d_start = pl.multiple_of(i * tq, tq)
    p_start = pl.multiple_of(jnp.maximum(i - 1, 0) * tq, tq)
    in_a = lax.broadcasted_iota(jnp.int32, (1, LANES), 1) < HEAD_DIM
    row = lax.broadcasted_iota(jnp.int32, (tq, tq), 0)
    col = lax.broadcasted_iota(jnp.int32, (tq, tq), 1)
    suffix = (row >= col).astype(BF16)
    row2 = lax.broadcasted_iota(jnp.int32, (2 * tq, tq), 0) % tq
    col2 = lax.broadcasted_iota(jnp.int32, (2 * tq, tq), 1)
    strict = col2 < row2

    def stacked_q(sl):
        q2 = q_ref[0, 0, :, sl]
        zero = jnp.zeros_like(q2)
        return jnp.concatenate([jnp.where(in_a, q2, zero), jnp.where(in_a, zero, q2)], axis=0)

    def softplus(z):
        return jnp.maximum(z, 0.0) + jnp.log2(1.0 + jnp.exp2(-jnp.abs(z)))

    def fold(pv):
        return jnp.where(in_a, pv[:tq], pv[tq:])

    for p in range(PAIRS):
        sl = slice(p * LANES, (p + 1) * LANES)
        qs = stacked_q(sl)
        z = _dot_nt(qs, k_ref[0, 0, pl.ds(d_start, tq), sl])
        sp = jnp.where(strict, softplus(z), 0.0)
        a = jnp.where(strict, jnp.exp2(z - _dot(sp.astype(BF16), suffix)), 0.0)
        pv = _dot(a.astype(BF16), v_ref[0, 0, pl.ds(d_start, tq), sl])
        r = jnp.sum(sp, axis=1, keepdims=True)
        z = _dot_nt(qs, k_ref[0, 0, pl.ds(p_start, tq), sl])
        sp = softplus(z)
        a = jnp.where(has_prev, jnp.exp2(z - _dot(sp.astype(BF16), suffix) - r), 0.0)
        pv = pv + _dot(a.astype(BF16), v_ref[0, 0, pl.ds(p_start, tq), sl])
        acc_ref[:, sl] = fold(pv)
        r_ref[p] = r + jnp.where(has_prev, jnp.sum(sp, axis=1, keepdims=True), 0.0)

    def more(state):
        j, mass = state[0], state[1]
        return jnp.logical_and(j >= 0, mass < SB_DEAD_MASS)

    for p in range(PAIRS):
        sl = slice(p * LANES, (p + 1) * LANES)

        def step(state, sl=sl):
            j, _, acc, r = state
            start = pl.multiple_of(j * tq, tq)
            z = _dot_nt(stacked_q(sl), k_ref[0, 0, pl.ds(start, tq), sl])
            sp = softplus(z)
            a = jnp.exp2(z - _dot(sp.astype(BF16), suffix) - r)
            acc = acc + fold(_dot(a.astype(BF16), v_ref[0, 0, pl.ds(start, tq), sl]))
            r = r + jnp.sum(sp, axis=1, keepdims=True)
            return (j - 1, jnp.min(r), acc, r)

        r = r_ref[p]
        state = lax.while_loop(more, step, (i - 2, jnp.min(r), acc_ref[:, sl], r))
        o_ref[0, :, sl] = state[2].astype(o_ref.dtype)


def _sb_attention(h, tq=256):
    bsz, _, s, _ = h.shape
    whole = lambda blk: pl.BlockSpec((1, 1, s, D_MODEL), lambda b, i: (b, 0, 0, blk), pipeline_mode=pl.Buffered(1))
    return pl.pallas_call(
        functools.partial(_sb_kernel, tq=tq),
        grid=(bsz, s // tq),
        in_specs=[pl.BlockSpec((1, 1, tq, D_MODEL), lambda b, i: (b, 0, i, 0)), whole(1), whole(2)],
        out_specs=pl.BlockSpec((1, tq, D_MODEL), lambda b, i: (b, i, 0)),
        out_shape=jax.ShapeDtypeStruct((bsz, s, D_MODEL), BF16),
        scratch_shapes=[pltpu.VMEM((tq, D_MODEL), F32), pltpu.VMEM((PAIRS, 2 * tq, 1), F32)],
        compiler_params=_cparams("parallel", "arbitrary"),
        name="sb_attention",
    )(h, h, h)


def _band_masks(n, max_dist):
    row = lax.broadcasted_iota(jnp.int32, (QUAD * BLK, 2 * BLK), 0)
    kj = lax.broadcasted_iota(jnp.int32, (QUAD * BLK, 2 * BLK), 1)
    diff = BLK + row % BLK - kj
    valid = (diff >= 0) & (diff <= max_dist) & ((kj >= BLK) | (n > 0))
    lane = lax.broadcasted_iota(jnp.int32, (1, QUAD_W), 1)
    qk_sel = [(lane % LANES) // HALF_DIM == hq for hq in range(QUAD)]
    in_a = lax.broadcasted_iota(jnp.int32, (1, LANES), 1) < HEAD_DIM
    return valid, qk_sel, in_a


def _band_quad(q4, kk, valid, qk_sel, sink=None):
    zero = jnp.zeros_like(q4)
    qs = jnp.concatenate([jnp.where(qk_sel[hq], q4, zero) for hq in range(QUAD)], axis=0)
    s = jnp.where(valid, _dot_nt(qs, kk), NEG_INF)
    m = jnp.max(s, axis=1, keepdims=True)
    if sink is not None:
        m = jnp.maximum(m, sink)
    e = jnp.exp2(s - m)
    den = jnp.sum(e, axis=1, keepdims=True)
    if sink is not None:
        den = den + jnp.exp2(sink - m)
    return e.astype(BF16), den, (m + jnp.log2(den)) * LN2


def _band_kernel(sink_ref, q_ref, kp_ref, kc_ref, vp_ref, vc_ref, o_ref, *, max_dist, kv_quad):
    valid, qk_sel, in_a = _band_masks(pl.program_id(1), max_dist)
    head_of_row = lax.broadcasted_iota(jnp.int32, (QUAD * BLK, 1), 0) // BLK
    for g in range(N_QUADS):
        c0 = kv_quad(g) * QUAD_W
        kk = jnp.concatenate([kp_ref[0, 0, :, c0:c0 + QUAD_W], kc_ref[0, 0, :, c0:c0 + QUAD_W]], axis=0)
        sink = jnp.full((QUAD * BLK, 1), sink_ref[QUAD * g] * LOG2E, F32)
        for hq in range(1, QUAD):
            sink = jnp.where(head_of_row == hq, sink_ref[QUAD * g + hq] * LOG2E, sink)
        eb, den, _ = _band_quad(q_ref[0, 0, :, g * QUAD_W:(g + 1) * QUAD_W], kk, valid, qk_sel, sink)
        for pr in range(QUAD // 2):
            c1 = c0 + pr * LANES
            rows = slice(2 * pr * BLK, 2 * (pr + 1) * BLK)
            vv = jnp.concatenate([vp_ref[0, 0, :, c1:c1 + LANES], vc_ref[0, 0, :, c1:c1 + LANES]], axis=0)
            o2 = _dot(eb[rows], vv) / den[rows]
            col = g * QUAD_W + pr * LANES
            o_ref[0, :, col:col + LANES] = jnp.where(in_a, o2[:BLK], o2[BLK:]).astype(o_ref.dtype)


def _band_attention(h, max_dist, kv_width, kv_quad, sinks):
    bsz, _, s, _ = h.shape
    k_blk = D_MODEL // kv_width
    cur = lambda blk: (lambda b, n: (b, 0, n, blk))
    prev = lambda blk: (lambda b, n: (b, 0, jnp.maximum(n - 1, 0), blk))
    kv_spec = lambda imap: pl.BlockSpec((1, 1, BLK, kv_width), imap)
    return pl.pallas_call(
        functools.partial(_band_kernel, max_dist=max_dist, kv_quad=kv_quad),
        grid=(bsz, s // BLK),
        in_specs=[pl.BlockSpec(memory_space=pltpu.SMEM), pl.BlockSpec((1, 1, BLK, D_MODEL), cur(0)),
                  kv_spec(prev(k_blk)), kv_spec(cur(k_blk)), kv_spec(prev(k_blk + 1)), kv_spec(cur(k_blk + 1))],
        out_specs=pl.BlockSpec((1, BLK, D_MODEL), lambda b, n: (b, n, 0)),
        out_shape=jax.ShapeDtypeStruct((bsz, s, D_MODEL), BF16),
        compiler_params=_cparams("parallel", "arbitrary"),
        name="band_attention",
    )(sinks, h, h, h, h, h)


def _band_dil_kernel(q_ref, kp_ref, kc_ref, vp_ref, vc_ref, o_ref, lse_ref, *, max_dist, dil, quads):
    valid, qk_sel, in_a = _band_masks(pl.program_id(1), max_dist)

    def one_class(r):
        dst = pl.ds(r, BLK, stride=dil) if dil > 1 else slice(None)
        for g in range(quads):
            cols = slice(g * QUAD_W, (g + 1) * QUAD_W)
            kk = jnp.concatenate([kp_ref[0, r, :, cols], kc_ref[0, r, :, cols]], axis=0)
            eb, den, lse = _band_quad(q_ref[0, r, :, cols], kk, valid, qk_sel)
            for pr in range(QUAD // 2):
                c1 = slice(g * QUAD_W + pr * LANES, g * QUAD_W + (pr + 1) * LANES)
                rows = slice(2 * pr * BLK, 2 * (pr + 1) * BLK)
                vv = jnp.concatenate([vp_ref[0, r, :, c1], vc_ref[0, r, :, c1]], axis=0)
                o2 = _dot(eb[rows], vv) / den[rows]
                l2 = lse[rows]
                o_ref[0, 2 * g + pr, dst, :] = jnp.where(in_a, o2[:BLK], o2[BLK:])
                lse_ref[0, 2 * g + pr, dst, :] = jnp.where(in_a, l2[:BLK], l2[BLK:])

    if dil == 1:
        one_class(0)
    else:
        def body(r2, carry):
            one_class(2 * r2)
            one_class(2 * r2 + 1)
            return carry
        lax.fori_loop(0, dil // 2, body, 0)


def _band_dil_attention(h, max_dist):
    bsz, dil, ls, _ = h.shape
    quads = N_QUADS if dil <= QUAD else N_QUADS // 2
    width = quads * QUAD_W
    nq = D_MODEL // width
    cur = lambda base: (lambda b, n, g: (b, 0, n, base + g))
    prev = lambda base: (lambda b, n, g: (b, 0, jnp.maximum(n - 1, 0), base + g))
    spec = lambda imap: pl.BlockSpec((1, dil, BLK, width), imap)
    o_spec = pl.BlockSpec((1, 2 * quads, BLK * dil, LANES), lambda b, n, g: (b, g, n, 0))
    o_shape = jax.ShapeDtypeStruct((bsz, PAIRS, ls * dil, LANES), F32)
    return pl.pallas_call(
        functools.partial(_band_dil_kernel, max_dist=max_dist, dil=dil, quads=quads),
        grid=(bsz, ls // BLK, nq),
        in_specs=[spec(cur(0)), spec(prev(nq)), spec(cur(nq)), spec(prev(2 * nq)), spec(cur(2 * nq))],
        out_specs=[o_spec, o_spec],
        out_shape=[o_shape, o_shape],
        compiler_params=_cparams("parallel", "arbitrary", "arbitrary"),
        name="band_dil_attention",
    )(h, h, h, h, h)


def _dil_out_ln_kernel(o0_ref, o1_ref, o2_ref, l0_ref, l1_ref, l2_ref, w_ref, x_ref, g_ref, b_ref, y_ref,
                       *, alpha):
    parts = []
    for p in range(PAIRS):
        l0, l1, l2 = l0_ref[0, p], l1_ref[0, p], l2_ref[0, p]
        m = jnp.maximum(jnp.maximum(l0, l1), l2)
        e0, e1, e2 = jnp.exp(l0 - m), jnp.exp(l1 - m), jnp.exp(l2 - m)
        merged = (e0 * o0_ref[0, p] + e1 * o1_ref[0, p] + e2 * o2_ref[0, p]) / (e0 + e1 + e2)
        parts.append(merged.astype(BF16))
    mix = _dot(jnp.concatenate(parts, axis=1), w_ref[...])
    y_ref[0] = _layernorm(alpha * x_ref[0] + mix, g_ref[...], b_ref[...])


def _dil_out_ln(outs, lses, w, layer, x, ln, alpha, tm=256):
    bsz, s, d = x.shape
    g, b, ln_row = ln
    slab = pl.BlockSpec((1, PAIRS, tm, LANES), lambda bi, i: (bi, 0, i, 0))
    row = pl.BlockSpec((1, tm, d), lambda bi, i: (bi, i, 0))
    return pl.pallas_call(
        functools.partial(_dil_out_ln_kernel, alpha=alpha),
        grid=(bsz, s // tm),
        in_specs=[slab] * 6 + [_layer_spec(w, layer), row, _layer_spec(g, ln_row), _layer_spec(b, ln_row)],
        out_specs=row,
        out_shape=jax.ShapeDtypeStruct((bsz, s, d), F32),
        compiler_params=_cparams("parallel", "parallel"),
        name="dil_out_ln",
    )(*outs, *lses, w, x, g, b)


def _quad_cols(w):
    d, n = w.shape
    w = w.reshape(d, n // QUAD_W, QUAD, 2, HALF_DIM).transpose(0, 1, 3, 2, 4)
    return w.reshape(d, n)


def _rope_tables(seq):
    pos = jnp.arange(seq, dtype=F32)
    inv = ROPE_THETA ** (-jnp.arange(0, HEAD_DIM, 2, dtype=F32) / HEAD_DIM)
    ang = pos[:, None] * jnp.tile(inv, LANES // HALF_DIM)[None, :]
    return jnp.cos(ang), jnp.sin(ang)


def kernel(x, p, ffn1_w_in, ffn1_w_out, ffn2_w_in, ffn2_w_out, ln_g, ln_b, sb_w_in, sb_w_out, swa_w_in,
           swa_sinks, swa_w_out, dil_w_in, dil_w_out, ple_w_proj, ple_w_gate):
    bsz, seq, d = x.shape
    depth = p.shape[0]
    t = bsz * seq
    alpha = (2 * depth) ** 0.25
    q_scale = HEAD_DIM ** -0.5 * LOG2E
    width = HEADS * HEAD_DIM
    tables = _rope_tables(seq)
    ffn1_wi, ffn2_wi = _cast_bf16(ffn1_w_in), _cast_bf16(ffn2_w_in)
    ffn1_wo, ffn2_wo = _cast_bf16(ffn1_w_out, 0.5), _cast_bf16(ffn2_w_out, 0.5)
    ple_gate, ple_proj = _cast_bf16(ple_w_gate), _cast_bf16(ple_w_proj)
    sb_wo, swa_wo, dil_wo = _cast_bf16(sb_w_out), _cast_bf16(swa_w_out), _cast_bf16(dil_w_out)
    p = p.reshape(depth, t, PLE_DIM)
    n_ln = ln_g.shape[1]
    ln_g = ln_g.reshape(depth * n_ln, 1, d)
    ln_b = ln_b.reshape(depth * n_ln, 1, d)
    x = x.reshape(t, d)
    for i in range(depth):
        kind, j = i % N_MIXERS, i // N_MIXERS
        ln = lambda k: (ln_g, ln_b, i * n_ln + k)
        x = _ffn(x, i, ffn1_wi, ffn1_wo, ln(0), alpha)
        x3d = x.reshape(bsz, seq, d)
        if kind == 0:
            w = sb_w_in[j]
            w = jnp.concatenate([w[:, :width] * q_scale, w[:, width:]], axis=1).astype(BF16)
            h = _proj(x3d, w, 1, (False,) * (w.shape[1] // QUAD_W))
            mix = (_sb_attention(h).reshape(t, width), sb_wo, j, ln(1))
        elif kind == 1:
            w = swa_w_in[j]
            kvw = SWA_KV_HEADS * HEAD_DIM
            wk = w[:, width:width + kvw].reshape(d, SWA_KV_HEADS, 2, 1, HALF_DIM)
            wv = w[:, width + kvw:].reshape(d, SWA_KV_HEADS, 1, HEAD_DIM)
            w = jnp.concatenate([_quad_cols(w[:, :width] * q_scale),
                                 jnp.repeat(wk, QUAD, axis=3).reshape(d, SWA_KV_HEADS * QUAD_W),
                                 jnp.repeat(wv, QUAD, axis=2).reshape(d, SWA_KV_HEADS * QUAD_W)],
                                axis=1).astype(BF16)
            roped = (True,) * (N_QUADS + SWA_KV_HEADS) + (False,) * SWA_KV_HEADS
            h = _proj(x3d, w, 1, roped, tables)
            o = _band_attention(h, SWA_WINDOW - 1, SWA_KV_HEADS * QUAD_W, lambda g: g // (N_QUADS // SWA_KV_HEADS),
                                swa_sinks[j])
            mix = (o.reshape(t, width), swa_wo, j, ln(1))
        else:
            outs, lses = [], []
            for gi, (win, dil) in enumerate(DIL_GROUPS):
                w = dil_w_in[j][:, gi * 3 * width:(gi + 1) * 3 * width]
                w = jnp.concatenate([_quad_cols(w[:, :width] * q_scale), _quad_cols(w[:, width:2 * width]),
                                     w[:, 2 * width:]], axis=1).astype(BF16)
                h = _proj(x3d, w, dil, (True,) * (2 * N_QUADS) + (False,) * N_QUADS, tables)
                o, lse = _band_dil_attention(h, win // dil)
                outs.append(o)
                lses.append(lse)
            x = _dil_out_ln(outs, lses, dil_wo, j, x3d, ln(1), alpha).reshape(t, d)
            mix = None
        x = _ffn(x, i, ffn2_wi, ffn2_wo, ln(2), alpha, ple=(p, ple_gate, ple_proj), mix=mix)
    return x.reshape(bsz, seq, d)
```

```python
import functools

import jax
import jax.numpy as jnp
from jax import lax
from jax.experimental import pallas as pl
from jax.experimental.pallas import tpu as pltpu

F32 = jnp.float32
BF16 = jnp.bfloat16

D_MODEL = 1024
D_FF = 2816
PLE_DIM = 256
LN_EPS = 1e-5
ROPE_THETA = 10000.0
N_MIXERS = 3
HEADS = 16
HEAD_DIM = 64
HALF_DIM = HEAD_DIM // 2
SWA_KV_HEADS = 2
SWA_WINDOW = 128
DIL_GROUPS = ((128, 1), (512, 4), (2048, 16))
BLK = 128
NEG_INF = -1e30
LOG2E = 1.4426950408889634
LN2 = 0.6931471805599453
SB_DEAD_MASS = 110.0 * LOG2E

LANES = 128
PAIRS = HEADS // 2
QUAD = 4
QUAD_W = QUAD * HEAD_DIM
N_QUADS = HEADS // QUAD
FF_CHUNK = 256
N_FF_CHUNKS = D_FF // FF_CHUNK
PROJ_CHUNK = 2 * QUAD_W
VMEM_LIMIT = 56 * 1024 * 1024


def _cparams(*sem):
    return pltpu.CompilerParams(dimension_semantics=sem, vmem_limit_bytes=VMEM_LIMIT)


def _dot(a, b):
    return jnp.dot(a, b, preferred_element_type=F32)


def _dot_nt(a, b):
    return lax.dot_general(a, b, (((1,), (1,)), ((), ())), preferred_element_type=F32)


def _layernorm(y, g, b):
    mu = jnp.mean(y, axis=-1, keepdims=True)
    yc = y - mu
    var = jnp.mean(yc * yc, axis=-1, keepdims=True)
    return yc * lax.rsqrt(var + LN_EPS) * g + b


def _sigmoid(x):
    return 1.0 / (1.0 + jnp.exp(-x))


def _const_spec(shape):
    nd = len(shape)
    return pl.BlockSpec(shape, lambda *_: (0,) * nd, pipeline_mode=pl.Buffered(1))


def _layer_spec(stack, layer):
    return pl.BlockSpec((None,) + stack.shape[1:], lambda *_: (layer, 0, 0), pipeline_mode=pl.Buffered(1))


def _cast_kernel(w_ref, o_ref, *, scale):
    w = w_ref[...]
    if scale != 1.0:
        w = w * scale
    o_ref[...] = w.astype(o_ref.dtype)


def _cast_bf16(w, scale=1.0, rows=256, n_layers=None):
    _, r, c = w.shape
    n_layers = w.shape[0] if n_layers is None else n_layers
    rb = min(rows, r)
    spec = pl.BlockSpec((1, rb, c), lambda a, b: (a, b, 0))
    return pl.pallas_call(
        functools.partial(_cast_kernel, scale=scale),
        grid=(n_layers, r // rb),
        in_specs=[spec],
        out_specs=spec,
        out_shape=jax.ShapeDtypeStruct((n_layers, r, c), BF16),
        compiler_params=_cparams("parallel", "parallel"),
        name="cast_bf16",
    )(w)


def _ffn_kernel(*refs, alpha, has_mix, has_ple, has_next):
    refs = list(refs)
    take = lambda n: [refs.pop(0) for _ in range(n)]
    if has_mix:
        a_ref, wm_ref, gm_ref, bm_ref = take(4)
    x_ref, wi_ref, wo_ref, g_ref, b_ref = take(5)
    if has_ple:
        p_ref, pg_ref, pp_ref = take(3)
    if has_next:
        nwi_ref, nwo_ref = take(2)
    (o_ref,) = take(1)
    if has_next:
        nwi_out, nwo_out = take(2)
        nwi_out[...] = nwi_ref[...].astype(BF16)
        nwo_out[...] = (0.5 * nwo_ref[...]).astype(BF16)
    (h_ref,) = take(1)

    if has_mix:
        mixed = alpha * x_ref[...] + _dot(a_ref[...], wm_ref[...])
        o_ref[...] = _layernorm(mixed, gm_ref[...], bm_ref[...])
        x_ref = o_ref
    xb = x_ref[...].astype(BF16)
    for c in range(N_FF_CHUNKS):
        lo = c * FF_CHUNK
        gate = _dot(xb, wi_ref[:, lo:lo + FF_CHUNK])
        up = _dot(xb, wi_ref[:, D_FF + lo:D_FF + lo + FF_CHUNK])
        h_ref[:, lo:lo + FF_CHUNK] = ((gate * _sigmoid(gate)) * up).astype(BF16)
    y = _layernorm(alpha * x_ref[...] + _dot(h_ref[...], wo_ref[...]), g_ref[...], b_ref[...])
    if has_ple:
        gate = _dot(y.astype(BF16), pg_ref[...])
        proj = _dot(p_ref[...].astype(BF16), pp_ref[...])
        y = y + _sigmoid(gate) * proj
    o_ref[...] = y


def _ffn(x, wi, wo, ln, alpha, ple=None, mix=None, nxt=None, tm=512):
    t = x.shape[0]
    steps = t // tm
    row = pl.BlockSpec((tm, D_MODEL), lambda i: (i, 0))
    g, b, ln_row = ln
    in_specs = [row, _const_spec(wi.shape), _const_spec(wo.shape), _layer_spec(g, ln_row), _layer_spec(b, ln_row)]
    args = [x, wi, wo, g, b]
    out_specs, out_shape = [row], [jax.ShapeDtypeStruct((t, D_MODEL), F32)]
    if mix is not None:
        a, wm, mix_layer, (gm, bm, mix_row) = mix
        in_specs = [row, _layer_spec(wm, mix_layer), _layer_spec(gm, mix_row), _layer_spec(bm, mix_row)] + in_specs
        args = [a, wm, gm, bm] + args
    if ple is not None:
        p, pg, pp, layer = ple
        in_specs += [pl.BlockSpec((None, tm, PLE_DIM), lambda i: (layer, i, 0)), _layer_spec(pg, layer),
                     _layer_spec(pp, layer)]
        args += [p, pg, pp]
    if nxt is not None:
        nwi, nwo, nlayer = nxt
        ri = nwi.shape[1] // steps
        so = steps // 2
        ro = nwo.shape[1] // so
        in_specs += [pl.BlockSpec((None, ri, nwi.shape[2]), lambda i: (nlayer, i, 0)),
                     pl.BlockSpec((None, ro, nwo.shape[2]), lambda i: (nlayer, jnp.minimum(i, so - 1), 0))]
        args += [nwi, nwo]
        out_specs += [pl.BlockSpec((ri, nwi.shape[2]), lambda i: (i, 0)),
                      pl.BlockSpec((ro, nwo.shape[2]), lambda i: (jnp.minimum(i, so - 1), 0))]
        out_shape += [jax.ShapeDtypeStruct(nwi.shape[1:], BF16), jax.ShapeDtypeStruct(nwo.shape[1:], BF16)]
    res = pl.pallas_call(
        functools.partial(_ffn_kernel, alpha=alpha, has_mix=mix is not None, has_ple=ple is not None,
                          has_next=nxt is not None),
        grid=(steps,),
        in_specs=in_specs,
        out_specs=out_specs,
        out_shape=out_shape,
        scratch_shapes=[pltpu.VMEM((tm, D_FF), BF16)],
        compiler_params=_cparams("arbitrary"),
        name="ffn",
    )(*args)
    return res[0], tuple(res[1:])


def _proj_kernel(*refs, roped, dil):
    refs = list(refs)
    slab_ref = refs.pop() if dil > 1 else None
    if any(roped):
        x_ref, w_ref, cos_ref, sin_ref, o_ref = refs
    else:
        x_ref, w_ref, o_ref = refs
    tm = x_ref.shape[1]
    rows = tm // dil

    def by_class(ref128):
        return jnp.concatenate([ref128[pl.ds(r, rows, stride=dil), :] for r in range(dil)], axis=0)

    if dil > 1:
        n_slabs = x_ref.shape[2] // LANES
        for s in range(n_slabs):
            slab_ref[s] = x_ref[0, :, s * LANES:(s + 1) * LANES]
        x = jnp.concatenate([by_class(slab_ref.at[s]) for s in range(n_slabs)], axis=1)
    else:
        x = x_ref[0]
    if any(roped):
        cos, sin = (by_class(cos_ref), by_class(sin_ref)) if dil > 1 else (cos_ref[...], sin_ref[...])

    def store(col, val):
        for r in range(dil):
            o_ref[0, r, :, col:col + val.shape[1]] = val[r * rows:(r + 1) * rows].astype(o_ref.dtype)

    xb = x.astype(BF16)
    per_chunk = PROJ_CHUNK // QUAD_W
    for c in range(len(roped) // per_chunk):
        acc = _dot(xb, w_ref[:, c * PROJ_CHUNK:(c + 1) * PROJ_CHUNK])
        for k in range(per_chunk):
            lo = k * QUAD_W
            col = c * PROJ_CHUNK + lo
            if roped[c * per_chunk + k]:
                x1, x2 = acc[:, lo:lo + LANES], acc[:, lo + LANES:lo + QUAD_W]
                store(col, x1 * cos - x2 * sin)
                store(col + LANES, x2 * cos + x1 * sin)
            else:
                store(col, acc[:, lo:lo + QUAD_W])


def _proj(x, w, dil, roped, tables=None, tm=512):
    bsz, s, d = x.shape
    n = w.shape[1]
    in_specs = [pl.BlockSpec((1, tm, d), lambda b, i: (b, i, 0)), _const_spec(w.shape)]
    args = [x, w]
    if any(roped):
        tab_spec = pl.BlockSpec((tm, LANES), lambda b, i: (i, 0))
        in_specs += [tab_spec, tab_spec]
        args += list(tables)
    return pl.pallas_call(
        functools.partial(_proj_kernel, roped=tuple(roped), dil=dil),
        grid=(bsz, s // tm),
        in_specs=in_specs,
        out_specs=pl.BlockSpec((1, dil, tm // dil, n), lambda b, i: (b, 0, i, 0)),
        out_shape=jax.ShapeDtypeStruct((bsz, dil, s // dil, n), BF16),
        scratch_shapes=[pltpu.VMEM((d // LANES, tm, LANES), F32)] if dil > 1 else [],
        compiler_params=_cparams("parallel", "parallel"),
        name="mixer_proj",
    )(*args)


def _sb_kernel(q_ref, k_ref, v_ref, o_ref, acc_ref, r_ref, *, tq):
    i = pl.program_id(1)
    has_prev = i > 0
    d_start = pl.multiple_of(i * tq, tq)
    p_start = pl.multiple_of(jnp.maximum(i - 1, 0) * tq, tq)
    in_a = lax.broadcasted_iota(jnp.int32, (1, LANES), 1) < HEAD_DIM
    row = lax.broadcasted_iota(jnp.int32, (tq, tq), 0)
    col = lax.broadcasted_iota(jnp.int32, (tq, tq), 1)
    suffix = (row >= col).astype(BF16)
    row2 = lax.broadcasted_iota(jnp.int32, (2 * tq, tq), 0) % tq
    col2 = lax.broadcasted_iota(jnp.int32, (2 * tq, tq), 1)
    strict = col2 < row2

    def stacked_q(sl):
        q2 = q_ref[0, 0, :, sl]
        zero = jnp.zeros_like(q2)
        return jnp.concatenate([jnp.where(in_a, q2, zero), jnp.where(in_a, zero, q2)], axis=0)

    def softplus(z):
        return jnp.maximum(z, 0.0) + jnp.log2(1.0 + jnp.exp2(-jnp.abs(z)))

    def fold(pv):
        return jnp.where(in_a, pv[:tq], pv[tq:])

    for p in range(PAIRS):
        sl = slice(p * LANES, (p + 1) * LANES)
        qs = stacked_q(sl)
        z = _dot_nt(qs, k_ref[0, 0, pl.ds(d_start, tq), sl])
        sp = jnp.where(strict, softplus(z), 0.0)
        a = jnp.where(strict, jnp.exp2(z - _dot(sp.astype(BF16), suffix)), 0.0)
        pv = _dot(a.astype(BF16), v_ref[0, 0, pl.ds(d_start, tq), sl])
        r = jnp.sum(sp, axis=1, keepdims=True)
        z = _dot_nt(qs, k_ref[0, 0, pl.ds(p_start, tq), sl])
        sp = softplus(z)
        a = jnp.where(has_prev, jnp.exp2(z - _dot(sp.astype(BF16), suffix) - r), 0.0)
        pv = pv + _dot(a.astype(BF16), v_ref[0, 0, pl.ds(p_start, tq), sl])
        acc_ref[:, sl] = fold(pv)
        r_ref[p] = r + jnp.where(has_prev, jnp.sum(sp, axis=1, keepdims=True), 0.0)

    def more(state):
        j, mass = state[0], state[1]
        return jnp.logical_and(j >= 0, mass < SB_DEAD_MASS)

    for p in range(PAIRS):
        sl = slice(p * LANES, (p + 1) * LANES)

        def step(state, sl=sl):
            j, _, acc, r = state
            start = pl.multiple_of(j * tq, tq)
            z = _dot_nt(stacked_q(sl), k_ref[0, 0, pl.ds(start, tq), sl])
            sp = softplus(z)
            a = jnp.exp2(z - _dot(sp.astype(BF16), suffix) - r)
            acc = acc + fold(_dot(a.astype(BF16), v_ref[0, 0, pl.ds(start, tq), sl]))
            r = r + jnp.sum(sp, axis=1, keepdims=True)
            return (j - 1, jnp.min(r), acc, r)

        r = r_ref[p]
        state = lax.while_loop(more, step, (i - 2, jnp.min(r), acc_ref[:, sl], r))
        o_ref[0, :, sl] = state[2].astype(o_ref.dtype)


def _sb_attention(h, tq=256):
    bsz, _, s, _ = h.shape
    whole = lambda blk: pl.BlockSpec((1, 1, s, D_MODEL), lambda b, i: (b, 0, 0, blk), pipeline_mode=pl.Buffered(1))
    return pl.pallas_call(
        functools.partial(_sb_kernel, tq=tq),
        grid=(bsz, s // tq),
        in_specs=[pl.BlockSpec((1, 1, tq, D_MODEL), lambda b, i: (b, 0, i, 0)), whole(1), whole(2)],
        out_specs=pl.BlockSpec((1, tq, D_MODEL), lambda b, i: (b, i, 0)),
        out_shape=jax.ShapeDtypeStruct((bsz, s, D_MODEL), BF16),
        scratch_shapes=[pltpu.VMEM((tq, D_MODEL), F32), pltpu.VMEM((PAIRS, 2 * tq, 1), F32)],
        compiler_params=_cparams("parallel", "arbitrary"),
        name="sb_attention",
    )(h, h, h)


def _band_masks(n, max_dist):
    row = lax.broadcasted_iota(jnp.int32, (QUAD * BLK, 2 * BLK), 0)
    kj = lax.broadcasted_iota(jnp.int32, (QUAD * BLK, 2 * BLK), 1)
    diff = BLK + row % BLK - kj
    valid = (diff >= 0) & (diff <= max_dist) & ((kj >= BLK) | (n > 0))
    lane = lax.broadcasted_iota(jnp.int32, (1, QUAD_W), 1)
    qk_sel = [(lane % LANES) // HALF_DIM == hq for hq in range(QUAD)]
    in_a = lax.broadcasted_iota(jnp.int32, (1, LANES), 1) < HEAD_DIM
    return valid, qk_sel, in_a


def _band_quad(q4, kk, valid, qk_sel, sink=None):
    zero = jnp.zeros_like(q4)
    qs = jnp.concatenate([jnp.where(qk_sel[hq], q4, zero) for hq in range(QUAD)], axis=0)
    s = jnp.where(valid, _dot_nt(qs, kk), NEG_INF)
    m = jnp.max(s, axis=1, keepdims=True)
    if sink is not None:
        m = jnp.maximum(m, sink)
    e = jnp.exp2(s - m)
    den = jnp.sum(e, axis=1, keepdims=True)
    if sink is not None:
        den = den + jnp.exp2(sink - m)
    return e.astype(BF16), den, (m + jnp.log2(den)) * LN2


def _band_kernel(sink_ref, q_ref, kp_ref, kc_ref, vp_ref, vc_ref, o_ref, *, max_dist, kv_quad):
    valid, qk_sel, in_a = _band_masks(pl.program_id(1), max_dist)
    head_of_row = lax.broadcasted_iota(jnp.int32, (QUAD * BLK, 1), 0) // BLK
    for g in range(N_QUADS):
        c0 = kv_quad(g) * QUAD_W
        kk = jnp.concatenate([kp_ref[0, 0, :, c0:c0 + QUAD_W], kc_ref[0, 0, :, c0:c0 + QUAD_W]], axis=0)
        sink = jnp.full((QUAD * BLK, 1), sink_ref[QUAD * g] * LOG2E, F32)
        for hq in range(1, QUAD):
            sink = jnp.where(head_of_row == hq, sink_ref[QUAD * g + hq] * LOG2E, sink)
        eb, den, _ = _band_quad(q_ref[0, 0, :, g * QUAD_W:(g + 1) * QUAD_W], kk, valid, qk_sel, sink)
        for pr in range(QUAD // 2):
            c1 = c0 + pr * LANES
            rows = slice(2 * pr * BLK, 2 * (pr + 1) * BLK)
            vv = jnp.concatenate([vp_ref[0, 0, :, c1:c1 + LANES], vc_ref[0, 0, :, c1:c1 + LANES]], axis=0)
            o2 = _dot(eb[rows], vv) / den[rows]
            col = g * QUAD_W + pr * LANES
            o_ref[0, :, col:col + LANES] = jnp.where(in_a, o2[:BLK], o2[BLK:]).astype(o_ref.dtype)


def _band_attention(h, max_dist, kv_width, kv_quad, sinks):
    bsz, _, s, _ = h.shape
    k_blk = D_MODEL // kv_width
    cur = lambda blk: (lambda b, n: (b, 0, n, blk))
    prev = lambda blk: (lambda b, n: (b, 0, jnp.maximum(n - 1, 0), blk))
    kv_spec = lambda imap: pl.BlockSpec((1, 1, BLK, kv_width), imap)
    return pl.pallas_call(
        functools.partial(_band_kernel, max_dist=max_dist, kv_quad=kv_quad),
        grid=(bsz, s // BLK),
        in_specs=[pl.BlockSpec(memory_space=pltpu.SMEM), pl.BlockSpec((1, 1, BLK, D_MODEL), cur(0)),
                  kv_spec(prev(k_blk)), kv_spec(cur(k_blk)), kv_spec(prev(k_blk + 1)), kv_spec(cur(k_blk + 1))],
        out_specs=pl.BlockSpec((1, BLK, D_MODEL), lambda b, n: (b, n, 0)),
        out_shape=jax.ShapeDtypeStruct((bsz, s, D_MODEL), BF16),
        compiler_params=_cparams("parallel", "arbitrary"),
        name="band_attention",
    )(sinks, h, h, h, h, h)


def _band_dil_kernel(q_ref, kp_ref, kc_ref, vp_ref, vc_ref, o_ref, lse_ref, *, max_dist, dil, quads):
    valid, qk_sel, in_a = _band_masks(pl.program_id(1), max_dist)

    def one_class(r):
        dst = pl.ds(r, BLK, stride=dil) if dil > 1 else slice(None)
        for g in range(quads):
            cols = slice(g * QUAD_W, (g + 1) * QUAD_W)
            kk = jnp.concatenate([kp_ref[0, r, :, cols], kc_ref[0, r, :, cols]], axis=0)
            eb, den, lse = _band_quad(q_ref[0, r, :, cols], kk, valid, qk_sel)
            for pr in range(QUAD // 2):
                c1 = slice(g * QUAD_W + pr * LANES, g * QUAD_W + (pr + 1) * LANES)
                rows = slice(2 * pr * BLK, 2 * (pr + 1) * BLK)
                vv = jnp.concatenate([vp_ref[0, r, :, c1], vc_ref[0, r, :, c1]], axis=0)
                o2 = _dot(eb[rows], vv) / den[rows]
                l2 = lse[rows]
                o_ref[0, 2 * g + pr, dst, :] = jnp.where(in_a, o2[:BLK], o2[BLK:])
                lse_ref[0, 2 * g + pr, dst, :] = jnp.where(in_a, l2[:BLK], l2[BLK:])

    if dil == 1:
        one_class(0)
    else:
        def body(r2, carry):
            one_class(2 * r2)
            one_class(2 * r2 + 1)
            return carry
        lax.fori_loop(0, dil // 2, body, 0)


def _band_dil_attention(h, max_dist):
    bsz, dil, ls, _ = h.shape
    quads = N_QUADS if dil <= QUAD else N_QUADS // 2
    width = quads * QUAD_W
    nq = D_MODEL // width
    cur = lambda base: (lambda b, n, g: (b, 0, n, base + g))
    prev = lambda base: (lambda b, n, g: (b, 0, jnp.maximum(n - 1, 0), base + g))
    spec = lambda imap: pl.BlockSpec((1, dil, BLK, width), imap)
    o_spec = pl.BlockSpec((1, 2 * quads, BLK * dil, LANES), lambda b, n, g: (b, g, n, 0))
    o_shape = jax.ShapeDtypeStruct((bsz, PAIRS, ls * dil, LANES), F32)
    return pl.pallas_call(
        functools.partial(_band_dil_kernel, max_dist=max_dist, dil=dil, quads=quads),
        grid=(bsz, ls // BLK, nq),
        in_specs=[spec(cur(0)), spec(prev(nq)), spec(cur(nq)), spec(prev(2 * nq)), spec(cur(2 * nq))],
        out_specs=[o_spec, o_spec],
        out_shape=[o_shape, o_shape],
        compiler_params=_cparams("parallel", "arbitrary", "arbitrary"),
        name="band_dil_attention",
    )(h, h, h, h, h)


def _dil_out_ln_kernel(o0_ref, o1_ref, o2_ref, l0_ref, l1_ref, l2_ref, w_ref, x_ref, g_ref, b_ref, y_ref,
                       *, alpha):
    parts = []
    for p in range(PAIRS):
        l0, l1, l2 = l0_ref[0, p], l1_ref[0, p], l2_ref[0, p]
        m = jnp.maximum(jnp.maximum(l0, l1), l2)
        e0, e1, e2 = jnp.exp(l0 - m), jnp.exp(l1 - m), jnp.exp(l2 - m)
        merged = (e0 * o0_ref[0, p] + e1 * o1_ref[0, p] + e2 * o2_ref[0, p]) / (e0 + e1 + e2)
        parts.append(merged.astype(BF16))
    mix = _dot(jnp.concatenate(parts, axis=1), w_ref[...])
    y_ref[0] = _layernorm(alpha * x_ref[0] + mix, g_ref[...], b_ref[...])


def _dil_out_ln(outs, lses, w, layer, x, ln, alpha, tm=256):
    bsz, s, d = x.shape
    g, b, ln_row = ln
    slab = pl.BlockSpec((1, PAIRS, tm, LANES), lambda bi, i: (bi, 0, i, 0))
    row = pl.BlockSpec((1, tm, d), lambda bi, i: (bi, i, 0))
    return pl.pallas_call(
        functools.partial(_dil_out_ln_kernel, alpha=alpha),
        grid=(bsz, s // tm),
        in_specs=[slab] * 6 + [_layer_spec(w, layer), row, _layer_spec(g, ln_row), _layer_spec(b, ln_row)],
        out_specs=row,
        out_shape=jax.ShapeDtypeStruct((bsz, s, d), F32),
        compiler_params=_cparams("parallel", "parallel"),
        name="dil_out_ln",
    )(*outs, *lses, w, x, g, b)


def _quad_cols(w):
    d, n = w.shape
    w = w.reshape(d, n // QUAD_W, QUAD, 2, HALF_DIM).transpose(0, 1, 3, 2, 4)
    return w.reshape(d, n)


def _rope_tables(seq):
    pos = jnp.arange(seq, dtype=F32)
    inv = ROPE_THETA ** (-jnp.arange(0, HEAD_DIM, 2, dtype=F32) / HEAD_DIM)
    ang = pos[:, None] * jnp.tile(inv, LANES // HALF_DIM)[None, :]
    return jnp.cos(ang), jnp.sin(ang)


def kernel(x, p, ffn1_w_in, ffn1_w_out, ffn2_w_in, ffn2_w_out, ln_g, ln_b, sb_w_in, sb_w_out, swa_w_in,
           swa_sinks, swa_w_out, dil_w_in, dil_w_out, ple_w_proj, ple_w_gate):
    bsz, seq, d = x.shape
    depth = p.shape[0]
    t = bsz * seq
    alpha = (2 * depth) ** 0.25
    q_scale = HEAD_DIM ** -0.5 * LOG2E
    width = HEADS * HEAD_DIM
    tables = _rope_tables(seq)
    ffn_w = (_cast_bf16(ffn1_w_in, n_layers=1)[0], _cast_bf16(ffn1_w_out, 0.5, n_layers=1)[0])
    ple_gate, ple_proj = _cast_bf16(ple_w_gate), _cast_bf16(ple_w_proj)
    sb_wo, swa_wo, dil_wo = _cast_bf16(sb_w_out), _cast_bf16(swa_w_out), _cast_bf16(dil_w_out)
    p = p.reshape(depth, t, PLE_DIM)
    n_ln = ln_g.shape[1]
    ln_g = ln_g.reshape(depth * n_ln, 1, d)
    ln_b = ln_b.reshape(depth * n_ln, 1, d)
    x = x.reshape(t, d)
    for i in range(depth):
        kind, j = i % N_MIXERS, i // N_MIXERS
        ln = lambda k: (ln_g, ln_b, i * n_ln + k)
        x, ffn_w = _ffn(x, *ffn_w, ln(0), alpha, nxt=(ffn2_w_in, ffn2_w_out, i))
        x3d = x.reshape(bsz, seq, d)
        if kind == 0:
            w = sb_w_in[j]
            w = jnp.concatenate([w[:, :width] * q_scale, w[:, width:]], axis=1).astype(BF16)
            h = _proj(x3d, w, 1, (False,) * (w.shape[1] // QUAD_W))
            mix = (_sb_attention(h).reshape(t, width), sb_wo, j, ln(1))
        elif kind == 1:
            w = swa_w_in[j]
            kvw = SWA_KV_HEADS * HEAD_DIM
            wk = w[:, width:width + kvw].reshape(d, SWA_KV_HEADS, 2, 1, HALF_DIM)
            wv = w[:, width + kvw:].reshape(d, SWA_KV_HEADS, 1, HEAD_DIM)
            w = jnp.concatenate([_quad_cols(w[:, :width] * q_scale),
                                 jnp.repeat(wk, QUAD, axis=3).reshape(d, SWA_KV_HEADS * QUAD_W),
                                 jnp.repeat(wv, QUAD, axis=2).reshape(d, SWA_KV_HEADS * QUAD_W)],
                                axis=1).astype(BF16)
            roped = (True,) * (N_QUADS + SWA_KV_HEADS) + (False,) * SWA_KV_HEADS
            h = _proj(x3d, w, 1, roped, tables)
            o = _band_attention(h, SWA_WINDOW - 1, SWA_KV_HEADS * QUAD_W, lambda g: g // (N_QUADS // SWA_KV_HEADS),
                                swa_sinks[j])
            mix = (o.reshape(t, width), swa_wo, j, ln(1))
        else:
            outs, lses = [], []
            for gi, (win, dil) in enumerate(DIL_GROUPS):
                w = dil_w_in[j][:, gi * 3 * width:(gi + 1) * 3 * width]
                w = jnp.concatenate([_quad_cols(w[:, :width] * q_scale), _quad_cols(w[:, width:2 * width]),
                                     w[:, 2 * width:]], axis=1).astype(BF16)
                h = _proj(x3d, w, dil, (True,) * (2 * N_QUADS) + (False,) * N_QUADS, tables)
                o, lse = _band_dil_attention(h, win // dil)
                outs.append(o)
                lses.append(lse)
            x = _dil_out_ln(outs, lses, dil_wo, j, x3d, ln(1), alpha).reshape(t, d)
            mix = None
        nxt = (ffn1_w_in, ffn1_w_out, i + 1) if i + 1 < depth else None
        x, ffn_w = _ffn(x, *ffn_w, ln(2), alpha, ple=(p, ple_gate, ple_proj, i), mix=mix, nxt=nxt)
    return x.reshape(bsz, seq, d)
```

```python
import functools

import jax
import jax.numpy as jnp
import numpy as np
from jax import lax
from jax.experimental import pallas as pl
from jax.experimental.pallas import tpu as pltpu

F32 = jnp.float32
BF16 = jnp.bfloat16

D_MODEL = 1024
D_FF = 2816
PLE_DIM = 256
LN_EPS = 1e-5
ROPE_THETA = 10000.0
N_MIXERS = 3
HEADS = 16
HEAD_DIM = 64
HALF_DIM = HEAD_DIM // 2
SWA_KV_HEADS = 2
SWA_WINDOW = 128
DIL_GROUPS = ((128, 1), (512, 4), (2048, 16))
BLK = 128
NEG_INF = -1e30
LOG2E = 1.4426950408889634
LN2 = 0.6931471805599453
SB_DEAD_MASS = 110.0 * LOG2E

LANES = 128
PAIRS = HEADS // 2
QUAD = 4
QUAD_W = QUAD * HEAD_DIM
N_QUADS = HEADS // QUAD
FF_CHUNK = 256
N_FF_CHUNKS = D_FF // FF_CHUNK
PROJ_CHUNK = 2 * QUAD_W
VMEM_LIMIT = 56 * 1024 * 1024


def _cparams(*sem):
    return pltpu.CompilerParams(dimension_semantics=sem, vmem_limit_bytes=VMEM_LIMIT)


def _dot(a, b):
    return jnp.dot(a, b, preferred_element_type=F32)


def _dot_nt(a, b):
    return lax.dot_general(a, b, (((1,), (1,)), ((), ())), preferred_element_type=F32)


def _layernorm(y, g, b):
    mu = jnp.mean(y, axis=-1, keepdims=True)
    yc = y - mu
    var = jnp.mean(yc * yc, axis=-1, keepdims=True)
    return yc * lax.rsqrt(var + LN_EPS) * g + b


def _sigmoid(x):
    return 1.0 / (1.0 + jnp.exp(-x))


def _const_spec(shape):
    nd = len(shape)
    return pl.BlockSpec(shape, lambda *_: (0,) * nd, pipeline_mode=pl.Buffered(1))


def _layer_spec(stack, layer):
    return pl.BlockSpec((None,) + stack.shape[1:], lambda *_: (layer, 0, 0), pipeline_mode=pl.Buffered(1))


def _cast_kernel(w_ref, o_ref, *, scale):
    w = w_ref[...]
    if scale != 1.0:
        w = w * scale
    o_ref[...] = w.astype(o_ref.dtype)


def _cast_bf16(w, scale=1.0, rows=256, n_layers=None):
    _, r, c = w.shape
    n_layers = w.shape[0] if n_layers is None else n_layers
    rb = min(rows, r)
    spec = pl.BlockSpec((1, rb, c), lambda a, b: (a, b, 0))
    return pl.pallas_call(
        functools.partial(_cast_kernel, scale=scale),
        grid=(n_layers, r // rb),
        in_specs=[spec],
        out_specs=spec,
        out_shape=jax.ShapeDtypeStruct((n_layers, r, c), BF16),
        compiler_params=_cparams("parallel", "parallel"),
        name="cast_bf16",
    )(w)


def _ffn_kernel(*refs, alpha, has_mix, has_ple, has_next):
    refs = list(refs)
    take = lambda n: [refs.pop(0) for _ in range(n)]
    if has_mix:
        a_ref, wm_ref, gm_ref, bm_ref = take(4)
    x_ref, wi_ref, wo_ref, g_ref, b_ref = take(5)
    if has_ple:
        p_ref, pg_ref, pp_ref = take(3)
    if has_next:
        nwi_ref, nwo_ref = take(2)
    (o_ref,) = take(1)
    if has_next:
        nwi_out, nwo_out = take(2)
        nwi_out[...] = nwi_ref[...].astype(BF16)
        nwo_out[...] = (0.5 * nwo_ref[...]).astype(BF16)
    (h_ref,) = take(1)

    if has_mix:
        mixed = alpha * x_ref[...] + _dot(a_ref[...], wm_ref[...])
        o_ref[...] = _layernorm(mixed, gm_ref[...], bm_ref[...])
        x_ref = o_ref
    xb = x_ref[...].astype(BF16)
    for c in range(N_FF_CHUNKS):
        lo = c * FF_CHUNK
        gate = _dot(xb, wi_ref[:, lo:lo + FF_CHUNK])
        up = _dot(xb, wi_ref[:, D_FF + lo:D_FF + lo + FF_CHUNK])
        h_ref[:, lo:lo + FF_CHUNK] = ((gate * _sigmoid(gate)) * up).astype(BF16)
    y = _layernorm(alpha * x_ref[...] + _dot(h_ref[...], wo_ref[...]), g_ref[...], b_ref[...])
    if has_ple:
        gate = _dot(y.astype(BF16), pg_ref[...])
        proj = _dot(p_ref[...].astype(BF16), pp_ref[...])
        y = y + _sigmoid(gate) * proj
    o_ref[...] = y


def _ffn(x, wi, wo, ln, alpha, ple=None, mix=None, nxt=None, tm=512):
    t = x.shape[0]
    steps = t // tm
    row = pl.BlockSpec((tm, D_MODEL), lambda i: (i, 0))
    g, b, ln_row = ln
    in_specs = [row, _const_spec(wi.shape), _const_spec(wo.shape), _layer_spec(g, ln_row), _layer_spec(b, ln_row)]
    args = [x, wi, wo, g, b]
    out_specs, out_shape = [row], [jax.ShapeDtypeStruct((t, D_MODEL), F32)]
    if mix is not None:
        a, wm, mix_layer, (gm, bm, mix_row) = mix
        in_specs = [row, _layer_spec(wm, mix_layer), _layer_spec(gm, mix_row), _layer_spec(bm, mix_row)] + in_specs
        args = [a, wm, gm, bm] + args
    if ple is not None:
        p, pg, pp, layer = ple
        in_specs += [pl.BlockSpec((None, tm, PLE_DIM), lambda i: (layer, i, 0)), _layer_spec(pg, layer),
                     _layer_spec(pp, layer)]
        args += [p, pg, pp]
    if nxt is not None:
        nwi, nwo, nlayer = nxt
        ri = nwi.shape[1] // steps
        so = steps // 2
        ro = nwo.shape[1] // so
        in_specs += [pl.BlockSpec((None, ri, nwi.shape[2]), lambda i: (nlayer, i, 0)),
                     pl.BlockSpec((None, ro, nwo.shape[2]), lambda i: (nlayer, jnp.minimum(i, so - 1), 0))]
        args += [nwi, nwo]
        out_specs += [pl.BlockSpec((ri, nwi.shape[2]), lambda i: (i, 0)),
                      pl.BlockSpec((ro, nwo.shape[2]), lambda i: (jnp.minimum(i, so - 1), 0))]
        out_shape += [jax.ShapeDtypeStruct(nwi.shape[1:], BF16), jax.ShapeDtypeStruct(nwo.shape[1:], BF16)]
    res = pl.pallas_call(
        functools.partial(_ffn_kernel, alpha=alpha, has_mix=mix is not None, has_ple=ple is not None,
                          has_next=nxt is not None),
        grid=(steps,),
        in_specs=in_specs,
        out_specs=out_specs,
        out_shape=out_shape,
        scratch_shapes=[pltpu.VMEM((tm, D_FF), BF16)],
        compiler_params=_cparams("arbitrary"),
        name="ffn",
    )(*args)
    return res[0], tuple(res[1:])


def _proj_kernel(*refs, roped, dil):
    refs = list(refs)
    slab_ref = refs.pop() if dil > 1 else None
    if any(roped):
        x_ref, w_ref, cos_ref, sin_ref, o_ref = refs
    else:
        x_ref, w_ref, o_ref = refs
    tm = x_ref.shape[1]
    rows = tm // dil

    def by_class(ref128):
        return jnp.concatenate([ref128[pl.ds(r, rows, stride=dil), :] for r in range(dil)], axis=0)

    if dil > 1:
        n_slabs = x_ref.shape[2] // LANES
        for s in range(n_slabs):
            slab_ref[s] = x_ref[0, :, s * LANES:(s + 1) * LANES]
        x = jnp.concatenate([by_class(slab_ref.at[s]) for s in range(n_slabs)], axis=1)
    else:
        x = x_ref[0]
    if any(roped):
        cos, sin = (by_class(cos_ref), by_class(sin_ref)) if dil > 1 else (cos_ref[...], sin_ref[...])

    def store(col, val):
        for r in range(dil):
            o_ref[0, r, :, col:col + val.shape[1]] = val[r * rows:(r + 1) * rows].astype(o_ref.dtype)

    xb = x.astype(BF16)
    per_chunk = PROJ_CHUNK // QUAD_W
    for c in range(len(roped) // per_chunk):
        acc = _dot(xb, w_ref[:, c * PROJ_CHUNK:(c + 1) * PROJ_CHUNK])
        for k in range(per_chunk):
            lo = k * QUAD_W
            col = c * PROJ_CHUNK + lo
            if roped[c * per_chunk + k]:
                x1, x2 = acc[:, lo:lo + LANES], acc[:, lo + LANES:lo + QUAD_W]
                store(col, x1 * cos - x2 * sin)
                store(col + LANES, x2 * cos + x1 * sin)
            else:
                store(col, acc[:, lo:lo + QUAD_W])


def _proj(x, w, dil, roped, tables=None, tm=512):
    bsz, s, d = x.shape
    n = w.shape[1]
    in_specs = [pl.BlockSpec((1, tm, d), lambda b, i: (b, i, 0)), _const_spec(w.shape)]
    args = [x, w]
    if any(roped):
        tab_spec = pl.BlockSpec((tm, LANES), lambda b, i: (i, 0))
        in_specs += [tab_spec, tab_spec]
        args += list(tables)
    return pl.pallas_call(
        functools.partial(_proj_kernel, roped=tuple(roped), dil=dil),
        grid=(bsz, s // tm),
        in_specs=in_specs,
        out_specs=pl.BlockSpec((1, dil, tm // dil, n), lambda b, i: (b, 0, i, 0)),
        out_shape=jax.ShapeDtypeStruct((bsz, dil, s // dil, n), BF16),
        scratch_shapes=[pltpu.VMEM((d // LANES, tm, LANES), F32)] if dil > 1 else [],
        compiler_params=_cparams("parallel", "parallel"),
        name="mixer_proj",
    )(*args)


def _sb_kernel(q_ref, k_ref, v_ref, o_ref, acc_ref, r_ref, *, tq):
    i = pl.program_id(1)
    has_prev = i > 0
    d_start = pl.multiple_of(i * tq, tq)
    p_start = pl.multiple_of(jnp.maximum(i - 1, 0) * tq, tq)
    in_a = lax.broadcasted_iota(jnp.int32, (1, LANES), 1) < HEAD_DIM
    row = lax.broadcasted_iota(jnp.int32, (tq, tq), 0)
    col = lax.broadcasted_iota(jnp.int32, (tq, tq), 1)
    suffix = (row >= col).astype(BF16)
    row2 = lax.broadcasted_iota(jnp.int32, (2 * tq, tq), 0) % tq
    col2 = lax.broadcasted_iota(jnp.int32, (2 * tq, tq), 1)
    strict = col2 < row2

    def stacked_q(sl):
        q2 = q_ref[0, 0, :, sl]
        zero = jnp.zeros_like(q2)
        return jnp.concatenate([jnp.where(in_a, q2, zero), jnp.where(in_a, zero, q2)], axis=0)

    def softplus(z):
        return jnp.maximum(z, 0.0) + jnp.log2(1.0 + jnp.exp2(-jnp.abs(z)))

    def fold(pv):
        return jnp.where(in_a, pv[:tq], pv[tq:])

    for p in range(PAIRS):
        sl = slice(p * LANES, (p + 1) * LANES)
        qs = stacked_q(sl)
        z = _dot_nt(qs, k_ref[0, 0, pl.ds(d_start, tq), sl])
        sp = jnp.where(strict, softplus(z), 0.0)
        a = jnp.where(strict, jnp.exp2(z - _dot(sp.astype(BF16), suffix)), 0.0)
        pv = _dot(a.astype(BF16), v_ref[0, 0, pl.ds(d_start, tq), sl])
        r = jnp.sum(sp, axis=1, keepdims=True)
        z = _dot_nt(qs, k_ref[0, 0, pl.ds(p_start, tq), sl])
        sp = softplus(z)
        a = jnp.where(has_prev, jnp.exp2(z - _dot(sp.astype(BF16), suffix) - r), 0.0)
        pv = pv + _dot(a.astype(BF16), v_ref[0, 0, pl.ds(p_start, tq), sl])
        acc_ref[:, sl] = fold(pv)
        r_ref[p] = r + jnp.where(has_prev, jnp.sum(sp, axis=1, keepdims=True), 0.0)

    def more(state):
        j, mass = state[0], state[1]
        return jnp.logical_and(j >= 0, mass < SB_DEAD_MASS)

    for p in range(PAIRS):
        sl = slice(p * LANES, (p + 1) * LANES)

        def step(state, sl=sl):
            j, _, acc, r = state
            start = pl.multiple_of(j * tq, tq)
            z = _dot_nt(stacked_q(sl), k_ref[0, 0, pl.ds(start, tq), sl])
            sp = softplus(z)
            a = jnp.exp2(z - _dot(sp.astype(BF16), suffix) - r)
            acc = acc + fold(_dot(a.astype(BF16), v_ref[0, 0, pl.ds(start, tq), sl]))
            r = r + jnp.sum(sp, axis=1, keepdims=True)
            return (j - 1, jnp.min(r), acc, r)

        r = r_ref[p]
        state = lax.while_loop(more, step, (i - 2, jnp.min(r), acc_ref[:, sl], r))
        o_ref[0, :, sl] = state[2].astype(o_ref.dtype)


def _sb_attention(h, tq=256):
    bsz, _, s, _ = h.shape
    whole = lambda blk: pl.BlockSpec((1, 1, s, D_MODEL), lambda b, i: (b, 0, 0, blk), pipeline_mode=pl.Buffered(1))
    return pl.pallas_call(
        functools.partial(_sb_kernel, tq=tq),
        grid=(bsz, s // tq),
        in_specs=[pl.BlockSpec((1, 1, tq, D_MODEL), lambda b, i: (b, 0, i, 0)), whole(1), whole(2)],
        out_specs=pl.BlockSpec((1, tq, D_MODEL), lambda b, i: (b, i, 0)),
        out_shape=jax.ShapeDtypeStruct((bsz, s, D_MODEL), BF16),
        scratch_shapes=[pltpu.VMEM((tq, D_MODEL), F32), pltpu.VMEM((PAIRS, 2 * tq, 1), F32)],
        compiler_params=_cparams("parallel", "arbitrary"),
        name="sb_attention",
    )(h, h, h)


def _band_masks(n, max_dist):
    row = lax.broadcasted_iota(jnp.int32, (QUAD * BLK, 2 * BLK), 0)
    kj = lax.broadcasted_iota(jnp.int32, (QUAD * BLK, 2 * BLK), 1)
    diff = BLK + row % BLK - kj
    valid = (diff >= 0) & (diff <= max_dist) & ((kj >= BLK) | (n > 0))
    lane = lax.broadcasted_iota(jnp.int32, (1, QUAD_W), 1)
    qk_sel = [(lane % LANES) // HALF_DIM == hq for hq in range(QUAD)]
    in_a = lax.broadcasted_iota(jnp.int32, (1, LANES), 1) < HEAD_DIM
    return valid, qk_sel, in_a


def _band_quad(q4, kk, valid, qk_sel, sink=None):
    zero = jnp.zeros_like(q4)
    qs = jnp.concatenate([jnp.where(qk_sel[hq], q4, zero) for hq in range(QUAD)], axis=0)
    s = jnp.where(valid, _dot_nt(qs, kk), NEG_INF)
    m = jnp.max(s, axis=1, keepdims=True)
    if sink is not None:
        m = jnp.maximum(m, sink)
    e = jnp.exp2(s - m)
    den = jnp.sum(e, axis=1, keepdims=True)
    if sink is not None:
        den = den + jnp.exp2(sink - m)
    return e.astype(BF16), den, m * LN2 + jnp.log(den)


def _band_kernel(sink_ref, q_ref, kp_ref, kc_ref, vp_ref, vc_ref, o_ref, *, max_dist, kv_quad):
    valid, qk_sel, in_a = _band_masks(pl.program_id(1), max_dist)
    head_of_row = lax.broadcasted_iota(jnp.int32, (QUAD * BLK, 1), 0) // BLK
    for g in range(N_QUADS):
        c0 = kv_quad(g) * QUAD_W
        kk = jnp.concatenate([kp_ref[0, 0, :, c0:c0 + QUAD_W], kc_ref[0, 0, :, c0:c0 + QUAD_W]], axis=0)
        sink = jnp.full((QUAD * BLK, 1), sink_ref[QUAD * g] * LOG2E, F32)
        for hq in range(1, QUAD):
            sink = jnp.where(head_of_row == hq, sink_ref[QUAD * g + hq] * LOG2E, sink)
        eb, den, _ = _band_quad(q_ref[0, 0, :, g * QUAD_W:(g + 1) * QUAD_W], kk, valid, qk_sel, sink)
        for pr in range(QUAD // 2):
            c1 = c0 + pr * LANES
            rows = slice(2 * pr * BLK, 2 * (pr + 1) * BLK)
            vv = jnp.concatenate([vp_ref[0, 0, :, c1:c1 + LANES], vc_ref[0, 0, :, c1:c1 + LANES]], axis=0)
            o2 = _dot(eb[rows], vv) / den[rows]
            col = g * QUAD_W + pr * LANES
            o_ref[0, :, col:col + LANES] = jnp.where(in_a, o2[:BLK], o2[BLK:]).astype(o_ref.dtype)


def _band_attention(h, max_dist, kv_width, kv_quad, sinks):
    bsz, _, s, _ = h.shape
    k_blk = D_MODEL // kv_width
    cur = lambda blk: (lambda b, n: (b, 0, n, blk))
    prev = lambda blk: (lambda b, n: (b, 0, jnp.maximum(n - 1, 0), blk))
    kv_spec = lambda imap: pl.BlockSpec((1, 1, BLK, kv_width), imap)
    return pl.pallas_call(
        functools.partial(_band_kernel, max_dist=max_dist, kv_quad=kv_quad),
        grid=(bsz, s // BLK),
        in_specs=[pl.BlockSpec(memory_space=pltpu.SMEM), pl.BlockSpec((1, 1, BLK, D_MODEL), cur(0)),
                  kv_spec(prev(k_blk)), kv_spec(cur(k_blk)), kv_spec(prev(k_blk + 1)), kv_spec(cur(k_blk + 1))],
        out_specs=pl.BlockSpec((1, BLK, D_MODEL), lambda b, n: (b, n, 0)),
        out_shape=jax.ShapeDtypeStruct((bsz, s, D_MODEL), BF16),
        compiler_params=_cparams("parallel", "arbitrary"),
        name="band_attention",
    )(sinks, h, h, h, h, h)


def _band_dil_kernel(q_ref, kp_ref, kc_ref, vp_ref, vc_ref, o_ref, lse_ref, *, max_dist, dil, quads):
    valid, qk_sel, in_a = _band_masks(pl.program_id(1), max_dist)

    def one_class(r):
        dst = pl.ds(r, BLK, stride=dil) if dil > 1 else slice(None)
        for g in range(quads):
            cols = slice(g * QUAD_W, (g + 1) * QUAD_W)
            kk = jnp.concatenate([kp_ref[0, r, :, cols], kc_ref[0, r, :, cols]], axis=0)
            eb, den, lse = _band_quad(q_ref[0, r, :, cols], kk, valid, qk_sel)
            for pr in range(QUAD // 2):
                c1 = slice(g * QUAD_W + pr * LANES, g * QUAD_W + (pr + 1) * LANES)
                rows = slice(2 * pr * BLK, 2 * (pr + 1) * BLK)
                vv = jnp.concatenate([vp_ref[0, r, :, c1], vc_ref[0, r, :, c1]], axis=0)
                o2 = _dot(eb[rows], vv) / den[rows]
                l2 = lse[rows]
                o_ref[0, 2 * g + pr, dst, :] = jnp.where(in_a, o2[:BLK], o2[BLK:])
                lse_ref[0, 2 * g + pr, dst, :] = jnp.where(in_a, l2[:BLK], l2[BLK:])

    if dil == 1:
        one_class(0)
    else:
        def body(r2, carry):
            one_class(2 * r2)
            one_class(2 * r2 + 1)
            return carry
        lax.fori_loop(0, dil // 2, body, 0)


def _band_dil_attention(h, max_dist):
    bsz, dil, ls, _ = h.shape
    quads = N_QUADS if dil <= QUAD else N_QUADS // 2
    width = quads * QUAD_W
    nq = D_MODEL // width
    cur = lambda base: (lambda b, n, g: (b, 0, n, base + g))
    prev = lambda base: (lambda b, n, g: (b, 0, jnp.maximum(n - 1, 0), base + g))
    spec = lambda imap: pl.BlockSpec((1, dil, BLK, width), imap)
    o_spec = pl.BlockSpec((1, 2 * quads, BLK * dil, LANES), lambda b, n, g: (b, g, n, 0))
    o_shape = jax.ShapeDtypeStruct((bsz, PAIRS, ls * dil, LANES), F32)
    return pl.pallas_call(
        functools.partial(_band_dil_kernel, max_dist=max_dist, dil=dil, quads=quads),
        grid=(bsz, ls // BLK, nq),
        in_specs=[spec(cur(0)), spec(prev(nq)), spec(cur(nq)), spec(prev(2 * nq)), spec(cur(2 * nq))],
        out_specs=[o_spec, o_spec],
        out_shape=[o_shape, o_shape],
        compiler_params=_cparams("parallel", "arbitrary", "arbitrary"),
        name="band_dil_attention",
    )(h, h, h, h, h)


def _dil_out_ln_kernel(o0_ref, o1_ref, o2_ref, l0_ref, l1_ref, l2_ref, w_ref, x_ref, g_ref, b_ref, y_ref,
                       *, alpha):
    parts = []
    for p in range(PAIRS):
        l0, l1, l2 = l0_ref[0, p], l1_ref[0, p], l2_ref[0, p]
        m = jnp.maximum(jnp.maximum(l0, l1), l2)
        e0, e1, e2 = jnp.exp(l0 - m), jnp.exp(l1 - m), jnp.exp(l2 - m)
        merged = (e0 * o0_ref[0, p] + e1 * o1_ref[0, p] + e2 * o2_ref[0, p]) / (e0 + e1 + e2)
        parts.append(merged.astype(BF16))
    mix = _dot(jnp.concatenate(parts, axis=1), w_ref[...])
    y_ref[0] = _layernorm(alpha * x_ref[0] + mix, g_ref[...], b_ref[...])


def _dil_out_ln(outs, lses, w, layer, x, ln, alpha, tm=256):
    bsz, s, d = x.shape
    g, b, ln_row = ln
    slab = pl.BlockSpec((1, PAIRS, tm, LANES), lambda bi, i: (bi, 0, i, 0))
    row = pl.BlockSpec((1, tm, d), lambda bi, i: (bi, i, 0))
    return pl.pallas_call(
        functools.partial(_dil_out_ln_kernel, alpha=alpha),
        grid=(bsz, s // tm),
        in_specs=[slab] * 6 + [_layer_spec(w, layer), row, _layer_spec(g, ln_row), _layer_spec(b, ln_row)],
        out_specs=row,
        out_shape=jax.ShapeDtypeStruct((bsz, s, d), F32),
        compiler_params=_cparams("parallel", "parallel"),
        name="dil_out_ln",
    )(*outs, *lses, w, x, g, b)


PERM_QUAD = 0
PERM_KDUP = 1
PERM_VDUP = 1 + SWA_KV_HEADS


def _selection_matrices():
    sel = np.zeros((1 + 2 * SWA_KV_HEADS, QUAD_W, QUAD_W), np.float32)
    dims = np.arange(HALF_DIM)
    for half in range(2):
        for hq in range(QUAD):
            sel[PERM_QUAD, hq * HEAD_DIM + half * HALF_DIM + dims, half * LANES + hq * HALF_DIM + dims] = 1.0
            for kv in range(SWA_KV_HEADS):
                sel[PERM_KDUP + kv, kv * HEAD_DIM + half * HALF_DIM + dims, half * LANES + hq * HALF_DIM + dims] = 1.0
    dims = np.arange(HEAD_DIM)
    for kv in range(SWA_KV_HEADS):
        for rep in range(QUAD):
            sel[PERM_VDUP + kv, SWA_KV_HEADS * HEAD_DIM + kv * HEAD_DIM + dims, rep * HEAD_DIM + dims] = 1.0
    return jnp.asarray(sel, BF16)


def _prep_kernel(w_ref, sel_ref, o_ref, *, plan):
    for j, (src, perm, scale) in enumerate(plan):
        blk = w_ref[:, src:src + QUAD_W]
        if scale != 1.0:
            blk = blk * scale
        blk = blk.astype(BF16)
        if perm is not None:
            blk = _dot(blk, sel_ref[perm]).astype(BF16)
        o_ref[:, j * QUAD_W:(j + 1) * QUAD_W] = blk


def _prep_w(w, layer, col_block, width, plan, sel, rows=256):
    d = w.shape[1]
    return pl.pallas_call(
        functools.partial(_prep_kernel, plan=tuple(plan)),
        grid=(d // rows,),
        in_specs=[pl.BlockSpec((None, rows, width), lambda i: (layer, i, col_block)), _const_spec(sel.shape)],
        out_specs=pl.BlockSpec((rows, len(plan) * QUAD_W), lambda i: (i, 0)),
        out_shape=jax.ShapeDtypeStruct((d, len(plan) * QUAD_W), BF16),
        compiler_params=_cparams("parallel"),
        name="prep_w",
    )(w, sel)


def _rope_tables(seq):
    pos = jnp.arange(seq, dtype=F32)
    inv = ROPE_THETA ** (-jnp.arange(0, HEAD_DIM, 2, dtype=F32) / HEAD_DIM)
    ang = pos[:, None] * jnp.tile(inv, LANES // HALF_DIM)[None, :]
    return jnp.cos(ang), jnp.sin(ang)


def kernel(x, p, ffn1_w_in, ffn1_w_out, ffn2_w_in, ffn2_w_out, ln_g, ln_b, sb_w_in, sb_w_out, swa_w_in,
           swa_sinks, swa_w_out, dil_w_in, dil_w_out, ple_w_proj, ple_w_gate):
    bsz, seq, d = x.shape
    depth = p.shape[0]
    t = bsz * seq
    alpha = (2 * depth) ** 0.25
    q_scale = HEAD_DIM ** -0.5 * LOG2E
    width = HEADS * HEAD_DIM
    tables = _rope_tables(seq)
    sel = _selection_matrices()
    ffn_w = (_cast_bf16(ffn1_w_in, n_layers=1)[0], _cast_bf16(ffn1_w_out, 0.5, n_layers=1)[0])
    ple_gate, ple_proj = _cast_bf16(ple_w_gate), _cast_bf16(ple_w_proj)
    sb_wo, swa_wo, dil_wo = _cast_bf16(sb_w_out), _cast_bf16(swa_w_out), _cast_bf16(dil_w_out)
    p = p.reshape(depth, t, PLE_DIM)
    n_ln = ln_g.shape[1]
    ln_g = ln_g.reshape(depth * n_ln, 1, d)
    ln_b = ln_b.reshape(depth * n_ln, 1, d)
    x = x.reshape(t, d)
    for i in range(depth):
        kind, j = i % N_MIXERS, i // N_MIXERS
        ln = lambda k: (ln_g, ln_b, i * n_ln + k)
        x, ffn_w = _ffn(x, *ffn_w, ln(0), alpha, nxt=(ffn2_w_in, ffn2_w_out, i))
        x3d = x.reshape(bsz, seq, d)
        if kind == 0:
            plan = [(QUAD_W * k, None, q_scale if k < N_QUADS else 1.0) for k in range(3 * N_QUADS)]
            w = _prep_w(sb_w_in, j, 0, 3 * width, plan, sel)
            h = _proj(x3d, w, 1, (False,) * (3 * N_QUADS))
            mix = (_sb_attention(h).reshape(t, width), sb_wo, j, ln(1))
        elif kind == 1:
            plan = ([(QUAD_W * k, PERM_QUAD, q_scale) for k in range(N_QUADS)]
                    + [(width, PERM_KDUP + kv, 1.0) for kv in range(SWA_KV_HEADS)]
                    + [(width, PERM_VDUP + kv, 1.0) for kv in range(SWA_KV_HEADS)])
            w = _prep_w(swa_w_in, j, 0, swa_w_in.shape[2], plan, sel)
            roped = (True,) * (N_QUADS + SWA_KV_HEADS) + (False,) * SWA_KV_HEADS
            h = _proj(x3d, w, 1, roped, tables)
            o = _band_attention(h, SWA_WINDOW - 1, SWA_KV_HEADS * QUAD_W, lambda g: g // (N_QUADS // SWA_KV_HEADS),
                                swa_sinks[j])
            mix = (o.reshape(t, width), swa_wo, j, ln(1))
        else:
            outs, lses = [], []
            for gi, (win, dil) in enumerate(DIL_GROUPS):
                plan = ([(QUAD_W * k, PERM_QUAD, q_scale) for k in range(N_QUADS)]
                        + [(QUAD_W * k, PERM_QUAD, 1.0) for k in range(N_QUADS, 2 * N_QUADS)]
                        + [(QUAD_W * k, None, 1.0) for k in range(2 * N_QUADS, 3 * N_QUADS)])
                w = _prep_w(dil_w_in, j, gi, 3 * width, plan, sel)
                h = _proj(x3d, w, dil, (True,) * (2 * N_QUADS) + (False,) * N_QUADS, tables)
                o, lse = _band_dil_attention(h, win // dil)
                outs.append(o)
                lses.append(lse)
            x = _dil_out_ln(outs, lses, dil_wo, j, x3d, ln(1), alpha).reshape(t, d)
            mix = None
        nxt = (ffn1_w_in, ffn1_w_out, i + 1) if i + 1 < depth else None
        x, ffn_w = _ffn(x, *ffn_w, ln(2), alpha, ple=(p, ple_gate, ple_proj, i), mix=mix, nxt=nxt)
    return x.reshape(bsz, seq, d)
```

```python
import functools

import jax
import jax.numpy as jnp
import numpy as np
from jax import lax
from jax.experimental import pallas as pl
from jax.experimental.pallas import tpu as pltpu

F32 = jnp.float32
BF16 = jnp.bfloat16

D_MODEL = 1024
D_FF = 2816
PLE_DIM = 256
LN_EPS = 1e-5
ROPE_THETA = 10000.0
N_MIXERS = 3
HEADS = 16
HEAD_DIM = 64
HALF_DIM = HEAD_DIM // 2
SWA_KV_HEADS = 2
SWA_WINDOW = 128
DIL_GROUPS = ((128, 1), (512, 4), (2048, 16))
BLK = 128
NEG_INF = -1e30
LOG2E = 1.4426950408889634
LN2 = 0.6931471805599453
SB_DEAD_MASS = 110.0 * LOG2E

LANES = 128
PAIRS = HEADS // 2
QUAD = 4
QUAD_W = QUAD * HEAD_DIM
N_QUADS = HEADS // QUAD
FF_CHUNK = 256
N_FF_CHUNKS = D_FF // FF_CHUNK
PROJ_CHUNK = 2 * QUAD_W
VMEM_LIMIT = 56 * 1024 * 1024


def _cparams(*sem):
    return pltpu.CompilerParams(dimension_semantics=sem, vmem_limit_bytes=VMEM_LIMIT)


def _dot(a, b):
    return jnp.dot(a, b, preferred_element_type=F32)


def _dot_nt(a, b):
    return lax.dot_general(a, b, (((1,), (1,)), ((), ())), preferred_element_type=F32)


def _layernorm(y, g, b):
    mu = jnp.mean(y, axis=-1, keepdims=True)
    yc = y - mu
    var = jnp.mean(yc * yc, axis=-1, keepdims=True)
    return yc * lax.rsqrt(var + LN_EPS) * g + b


def _sigmoid(x):
    return 1.0 / (1.0 + jnp.exp(-x))


def _const_spec(shape):
    nd = len(shape)
    return pl.BlockSpec(shape, lambda *_: (0,) * nd, pipeline_mode=pl.Buffered(1))


def _layer_spec(stack, layer):
    return pl.BlockSpec((None,) + stack.shape[1:], lambda *_: (layer, 0, 0), pipeline_mode=pl.Buffered(1))


def _cast_kernel(w_ref, o_ref, *, scale):
    w = w_ref[...]
    if scale != 1.0:
        w = w * scale
    o_ref[...] = w.astype(o_ref.dtype)


def _cast_bf16(w, scale=1.0, rows=256, n_layers=None):
    _, r, c = w.shape
    n_layers = w.shape[0] if n_layers is None else n_layers
    rb = min(rows, r)
    spec = pl.BlockSpec((1, rb, c), lambda a, b: (a, b, 0))
    return pl.pallas_call(
        functools.partial(_cast_kernel, scale=scale),
        grid=(n_layers, r // rb),
        in_specs=[spec],
        out_specs=spec,
        out_shape=jax.ShapeDtypeStruct((n_layers, r, c), BF16),
        compiler_params=_cparams("parallel", "parallel"),
        name="cast_bf16",
    )(w)


def _ffn_kernel(*refs, alpha, has_mix, has_ple, has_next):
    refs = list(refs)
    take = lambda n: [refs.pop(0) for _ in range(n)]
    if has_mix:
        a_ref, wm_ref, gm_ref, bm_ref = take(4)
    x_ref, wi_ref, wo_ref, g_ref, b_ref = take(5)
    if has_ple:
        p_ref, pg_ref, pp_ref = take(3)
    if has_next:
        nwi_ref, nwo_ref = take(2)
    (o_ref,) = take(1)
    if has_next:
        nwi_out, nwo_out = take(2)
        nwi_out[...] = nwi_ref[...].astype(BF16)
        nwo_out[...] = (0.5 * nwo_ref[...]).astype(BF16)
    (h_ref,) = take(1)

    if has_mix:
        mixed = alpha * x_ref[...] + _dot(a_ref[...], wm_ref[...])
        o_ref[...] = _layernorm(mixed, gm_ref[...], bm_ref[...])
        x_ref = o_ref
    xb = x_ref[...].astype(BF16)
    for c in range(N_FF_CHUNKS):
        lo = c * FF_CHUNK
        gate = _dot(xb, wi_ref[:, lo:lo + FF_CHUNK])
        up = _dot(xb, wi_ref[:, D_FF + lo:D_FF + lo + FF_CHUNK])
        h_ref[:, lo:lo + FF_CHUNK] = ((gate * _sigmoid(gate)) * up).astype(BF16)
    y = _layernorm(alpha * x_ref[...] + _dot(h_ref[...], wo_ref[...]), g_ref[...], b_ref[...])
    if has_ple:
        gate = _dot(y.astype(BF16), pg_ref[...])
        proj = _dot(p_ref[...].astype(BF16), pp_ref[...])
        y = y + _sigmoid(gate) * proj
    o_ref[...] = y


def _ffn(x, wi, wo, ln, alpha, ple=None, mix=None, nxt=None, tm=512):
    t = x.shape[0]
    steps = t // tm
    row = pl.BlockSpec((tm, D_MODEL), lambda i: (i, 0))
    g, b, ln_row = ln
    in_specs = [row, _const_spec(wi.shape), _const_spec(wo.shape), _layer_spec(g, ln_row), _layer_spec(b, ln_row)]
    args = [x, wi, wo, g, b]
    out_specs, out_shape = [row], [jax.ShapeDtypeStruct((t, D_MODEL), F32)]
    if mix is not None:
        a, wm, mix_layer, (gm, bm, mix_row) = mix
        in_specs = [row, _layer_spec(wm, mix_layer), _layer_spec(gm, mix_row), _layer_spec(bm, mix_row)] + in_specs
        args = [a, wm, gm, bm] + args
    if ple is not None:
        p, pg, pp, layer = ple
        in_specs += [pl.BlockSpec((None, tm, PLE_DIM), lambda i: (layer, i, 0)), _layer_spec(pg, layer),
                     _layer_spec(pp, layer)]
        args += [p, pg, pp]
    if nxt is not None:
        nwi, nwo, nlayer = nxt
        ri = nwi.shape[1] // steps
        so = steps // 2
        ro = nwo.shape[1] // so
        in_specs += [pl.BlockSpec((None, ri, nwi.shape[2]), lambda i: (nlayer, i, 0)),
                     pl.BlockSpec((None, ro, nwo.shape[2]), lambda i: (nlayer, jnp.minimum(i, so - 1), 0))]
        args += [nwi, nwo]
        out_specs += [pl.BlockSpec((ri, nwi.shape[2]), lambda i: (i, 0)),
                      pl.BlockSpec((ro, nwo.shape[2]), lambda i: (jnp.minimum(i, so - 1), 0))]
        out_shape += [jax.ShapeDtypeStruct(nwi.shape[1:], BF16), jax.ShapeDtypeStruct(nwo.shape[1:], BF16)]
    res = pl.pallas_call(
        functools.partial(_ffn_kernel, alpha=alpha, has_mix=mix is not None, has_ple=ple is not None,
                          has_next=nxt is not None),
        grid=(steps,),
        in_specs=in_specs,
        out_specs=out_specs,
        out_shape=out_shape,
        scratch_shapes=[pltpu.VMEM((tm, D_FF), BF16)],
        compiler_params=_cparams("arbitrary"),
        name="ffn",
    )(*args)
    return res[0], tuple(res[1:])


def _proj_kernel(*refs, roped, dil):
    refs = list(refs)
    slab_ref = refs.pop() if dil > 1 else None
    if any(roped):
        x_ref, w_ref, cos_ref, sin_ref, o_ref = refs
    else:
        x_ref, w_ref, o_ref = refs
    tm = x_ref.shape[1]
    rows = tm // dil

    def by_class(ref128):
        return jnp.concatenate([ref128[pl.ds(r, rows, stride=dil), :] for r in range(dil)], axis=0)

    if dil > 1:
        n_slabs = x_ref.shape[2] // LANES
        for s in range(n_slabs):
            slab_ref[s] = x_ref[0, :, s * LANES:(s + 1) * LANES]
        x = jnp.concatenate([by_class(slab_ref.at[s]) for s in range(n_slabs)], axis=1)
    else:
        x = x_ref[0]
    if any(roped):
        cos, sin = (by_class(cos_ref), by_class(sin_ref)) if dil > 1 else (cos_ref[...], sin_ref[...])

    def store(col, val):
        for r in range(dil):
            o_ref[0, r, :, col:col + val.shape[1]] = val[r * rows:(r + 1) * rows].astype(o_ref.dtype)

    xb = x.astype(BF16)
    per_chunk = PROJ_CHUNK // QUAD_W
    for c in range(len(roped) // per_chunk):
        acc = _dot(xb, w_ref[:, c * PROJ_CHUNK:(c + 1) * PROJ_CHUNK])
        for k in range(per_chunk):
            lo = k * QUAD_W
            col = c * PROJ_CHUNK + lo
            if roped[c * per_chunk + k]:
                x1, x2 = acc[:, lo:lo + LANES], acc[:, lo + LANES:lo + QUAD_W]
                store(col, x1 * cos - x2 * sin)
                store(col + LANES, x2 * cos + x1 * sin)
            else:
                store(col, acc[:, lo:lo + QUAD_W])


def _proj(x, w, dil, roped, tables=None, tm=512):
    bsz, s, d = x.shape
    n = w.shape[1]
    in_specs = [pl.BlockSpec((1, tm, d), lambda b, i: (b, i, 0)), _const_spec(w.shape)]
    args = [x, w]
    if any(roped):
        tab_spec = pl.BlockSpec((tm, LANES), lambda b, i: (i, 0))
        in_specs += [tab_spec, tab_spec]
        args += list(tables)
    return pl.pallas_call(
        functools.partial(_proj_kernel, roped=tuple(roped), dil=dil),
        grid=(bsz, s // tm),
        in_specs=in_specs,
        out_specs=pl.BlockSpec((1, dil, tm // dil, n), lambda b, i: (b, 0, i, 0)),
        out_shape=jax.ShapeDtypeStruct((bsz, dil, s // dil, n), BF16),
        scratch_shapes=[pltpu.VMEM((d // LANES, tm, LANES), F32)] if dil > 1 else [],
        compiler_params=_cparams("parallel", "parallel"),
        name="mixer_proj",
    )(*args)


def _sb_kernel(q_ref, k_ref, v_ref, o_ref, acc_ref, r_ref, *, tq):
    i = pl.program_id(1)
    has_prev = i > 0
    d_start = pl.multiple_of(i * tq, tq)
    p_start = pl.multiple_of(jnp.maximum(i - 1, 0) * tq, tq)
    in_a = lax.broadcasted_iota(jnp.int32, (1, LANES), 1) < HEAD_DIM
    row = lax.broadcasted_iota(jnp.int32, (tq, tq), 0)
    col = lax.broadcasted_iota(jnp.int32, (tq, tq), 1)
    suffix = (row >= col).astype(BF16)
    row2 = lax.broadcasted_iota(jnp.int32, (2 * tq, tq), 0) % tq
    col2 = lax.broadcasted_iota(jnp.int32, (2 * tq, tq), 1)
    strict = col2 < row2

    def stacked_q(sl):
        q2 = q_ref[0, 0, :, sl]
        zero = jnp.zeros_like(q2)
        return jnp.concatenate([jnp.where(in_a, q2, zero), jnp.where(in_a, zero, q2)], axis=0)

    def softplus(z):
        return jnp.maximum(z, 0.0) + jnp.log2(1.0 + jnp.exp2(-jnp.abs(z)))

    def fold(pv):
        return jnp.where(in_a, pv[:tq], pv[tq:])

    for p in range(PAIRS):
        sl = slice(p * LANES, (p + 1) * LANES)
        qs = stacked_q(sl)
        z = _dot_nt(qs, k_ref[0, 0, pl.ds(d_start, tq), sl])
        sp = jnp.where(strict, softplus(z), 0.0)
        a = jnp.where(strict, jnp.exp2(z - _dot(sp.astype(BF16), suffix)), 0.0)
        pv = _dot(a.astype(BF16), v_ref[0, 0, pl.ds(d_start, tq), sl])
        r = jnp.sum(sp, axis=1, keepdims=True)
        z = _dot_nt(qs, k_ref[0, 0, pl.ds(p_start, tq), sl])
        sp = softplus(z)
        a = jnp.where(has_prev, jnp.exp2(z - _dot(sp.astype(BF16), suffix) - r), 0.0)
        pv = pv + _dot(a.astype(BF16), v_ref[0, 0, pl.ds(p_start, tq), sl])
        acc_ref[:, sl] = fold(pv)
        r_ref[p] = r + jnp.where(has_prev, jnp.sum(sp, axis=1, keepdims=True), 0.0)

    def more(state):
        j, mass = state[0], state[1]
        return jnp.logical_and(j >= 0, mass < SB_DEAD_MASS)

    for p in range(PAIRS):
        sl = slice(p * LANES, (p + 1) * LANES)

        def step(state, sl=sl):
            j, _, acc, r = state
            start = pl.multiple_of(j * tq, tq)
            z = _dot_nt(stacked_q(sl), k_ref[0, 0, pl.ds(start, tq), sl])
            sp = softplus(z)
            a = jnp.exp2(z - _dot(sp.astype(BF16), suffix) - r)
            acc = acc + fold(_dot(a.astype(BF16), v_ref[0, 0, pl.ds(start, tq), sl]))
            r = r + jnp.sum(sp, axis=1, keepdims=True)
            return (j - 1, jnp.min(r), acc, r)

        r = r_ref[p]
        state = lax.while_loop(more, step, (i - 2, jnp.min(r), acc_ref[:, sl], r))
        o_ref[0, :, sl] = state[2].astype(o_ref.dtype)


def _sb_attention(h, tq=256):
    bsz, _, s, _ = h.shape
    whole = lambda blk: pl.BlockSpec((1, 1, s, D_MODEL), lambda b, i: (b, 0, 0, blk), pipeline_mode=pl.Buffered(1))
    return pl.pallas_call(
        functools.partial(_sb_kernel, tq=tq),
        grid=(bsz, s // tq),
        in_specs=[pl.BlockSpec((1, 1, tq, D_MODEL), lambda b, i: (b, 0, i, 0)), whole(1), whole(2)],
        out_specs=pl.BlockSpec((1, tq, D_MODEL), lambda b, i: (b, i, 0)),
        out_shape=jax.ShapeDtypeStruct((bsz, s, D_MODEL), BF16),
        scratch_shapes=[pltpu.VMEM((tq, D_MODEL), F32), pltpu.VMEM((PAIRS, 2 * tq, 1), F32)],
        compiler_params=_cparams("parallel", "arbitrary"),
        name="sb_attention",
    )(h, h, h)


def _band_bias(max_dist):
    row = lax.broadcasted_iota(jnp.int32, (QUAD * BLK, 2 * BLK), 0)
    kj = lax.broadcasted_iota(jnp.int32, (QUAD * BLK, 2 * BLK), 1)
    diff = BLK + row % BLK - kj
    band = (diff >= 0) & (diff <= max_dist)
    return jnp.where(jnp.stack([band & (kj >= BLK), band]), 0.0, NEG_INF).astype(F32)


def _band_lane_masks():
    lane = lax.broadcasted_iota(jnp.int32, (1, QUAD_W), 1)
    qk_sel = [(lane % LANES) // HALF_DIM == hq for hq in range(QUAD)]
    in_a = lax.broadcasted_iota(jnp.int32, (1, LANES), 1) < HEAD_DIM
    return qk_sel, in_a


def _band_quad(q4, kk, bias, qk_sel, sink=None):
    zero = jnp.zeros_like(q4)
    qs = jnp.concatenate([jnp.where(qk_sel[hq], q4, zero) for hq in range(QUAD)], axis=0)
    s = _dot_nt(qs, kk) + bias
    m = jnp.max(s, axis=1, keepdims=True)
    if sink is not None:
        m = jnp.maximum(m, sink)
    e = jnp.exp2(s - m)
    den = jnp.sum(e, axis=1, keepdims=True)
    if sink is not None:
        den = den + jnp.exp2(sink - m)
    return e.astype(BF16), den, m * LN2 + jnp.log(den)


def _band_kernel(sink_ref, bias_ref, q_ref, kp_ref, kc_ref, vp_ref, vc_ref, o_ref, *, kv_quad):
    qk_sel, in_a = _band_lane_masks()
    which = jnp.minimum(pl.program_id(1), 1)
    head_of_row = lax.broadcasted_iota(jnp.int32, (QUAD * BLK, 1), 0) // BLK
    for g in range(N_QUADS):
        c0 = kv_quad(g) * QUAD_W
        kk = jnp.concatenate([kp_ref[0, 0, :, c0:c0 + QUAD_W], kc_ref[0, 0, :, c0:c0 + QUAD_W]], axis=0)
        sink = jnp.full((QUAD * BLK, 1), sink_ref[QUAD * g] * LOG2E, F32)
        for hq in range(1, QUAD):
            sink = jnp.where(head_of_row == hq, sink_ref[QUAD * g + hq] * LOG2E, sink)
        eb, den, _ = _band_quad(q_ref[0, 0, :, g * QUAD_W:(g + 1) * QUAD_W], kk, bias_ref[which], qk_sel, sink)
        for pr in range(QUAD // 2):
            c1 = c0 + pr * LANES
            rows = slice(2 * pr * BLK, 2 * (pr + 1) * BLK)
            vv = jnp.concatenate([vp_ref[0, 0, :, c1:c1 + LANES], vc_ref[0, 0, :, c1:c1 + LANES]], axis=0)
            o2 = _dot(eb[rows], vv) / den[rows]
            col = g * QUAD_W + pr * LANES
            o_ref[0, :, col:col + LANES] = jnp.where(in_a, o2[:BLK], o2[BLK:]).astype(o_ref.dtype)


def _band_attention(h, max_dist, kv_width, kv_quad, sinks):
    bsz, _, s, _ = h.shape
    k_blk = D_MODEL // kv_width
    bias = _band_bias(max_dist)
    cur = lambda blk: (lambda b, n: (b, 0, n, blk))
    prev = lambda blk: (lambda b, n: (b, 0, jnp.maximum(n - 1, 0), blk))
    kv_spec = lambda imap: pl.BlockSpec((1, 1, BLK, kv_width), imap)
    return pl.pallas_call(
        functools.partial(_band_kernel, kv_quad=kv_quad),
        grid=(bsz, s // BLK),
        in_specs=[pl.BlockSpec(memory_space=pltpu.SMEM), _const_spec(bias.shape), pl.BlockSpec((1, 1, BLK, D_MODEL), cur(0)),
                  kv_spec(prev(k_blk)), kv_spec(cur(k_blk)), kv_spec(prev(k_blk + 1)), kv_spec(cur(k_blk + 1))],
        out_specs=pl.BlockSpec((1, BLK, D_MODEL), lambda b, n: (b, n, 0)),
        out_shape=jax.ShapeDtypeStruct((bsz, s, D_MODEL), BF16),
        compiler_params=_cparams("parallel", "arbitrary"),
        name="band_attention",
    )(sinks, bias, h, h, h, h, h)


def _band_dil_kernel(bias_ref, q_ref, kp_ref, kc_ref, vp_ref, vc_ref, o_ref, lse_ref, *, dil, quads):
    qk_sel, in_a = _band_lane_masks()
    which = jnp.minimum(pl.program_id(1), 1)

    def one_class(r):
        dst = pl.ds(r, BLK, stride=dil) if dil > 1 else slice(None)
        for g in range(quads):
            cols = slice(g * QUAD_W, (g + 1) * QUAD_W)
            kk = jnp.concatenate([kp_ref[0, r, :, cols], kc_ref[0, r, :, cols]], axis=0)
            eb, den, lse = _band_quad(q_ref[0, r, :, cols], kk, bias_ref[which], qk_sel)
            for pr in range(QUAD // 2):
                c1 = slice(g * QUAD_W + pr * LANES, g * QUAD_W + (pr + 1) * LANES)
                rows = slice(2 * pr * BLK, 2 * (pr + 1) * BLK)
                vv = jnp.concatenate([vp_ref[0, r, :, c1], vc_ref[0, r, :, c1]], axis=0)
                o2 = _dot(eb[rows], vv) / den[rows]
                l2 = lse[rows]
                o_ref[0, 2 * g + pr, dst, :] = jnp.where(in_a, o2[:BLK], o2[BLK:])
                lse_ref[0, 2 * g + pr, dst, :] = jnp.where(in_a, l2[:BLK], l2[BLK:])

    if dil == 1:
        one_class(0)
    else:
        def body(r2, carry):
            one_class(2 * r2)
            one_class(2 * r2 + 1)
            return carry
        lax.fori_loop(0, dil // 2, body, 0)


def _band_dil_attention(h, max_dist):
    bsz, dil, ls, _ = h.shape
    quads = N_QUADS if dil <= QUAD else N_QUADS // 2
    width = quads * QUAD_W
    nq = D_MODEL // width
    bias = _band_bias(max_dist)
    cur = lambda base: (lambda b, n, g: (b, 0, n, base + g))
    prev = lambda base: (lambda b, n, g: (b, 0, jnp.maximum(n - 1, 0), base + g))
    spec = lambda imap: pl.BlockSpec((1, dil, BLK, width), imap)
    o_spec = pl.BlockSpec((1, 2 * quads, BLK * dil, LANES), lambda b, n, g: (b, g, n, 0))
    o_shape = jax.ShapeDtypeStruct((bsz, PAIRS, ls * dil, LANES), F32)
    return pl.pallas_call(
        functools.partial(_band_dil_kernel, dil=dil, quads=quads),
        grid=(bsz, ls // BLK, nq),
        in_specs=[_const_spec(bias.shape), spec(cur(0)), spec(prev(nq)), spec(cur(nq)), spec(prev(2 * nq)),
                  spec(cur(2 * nq))],
        out_specs=[o_spec, o_spec],
        out_shape=[o_shape, o_shape],
        compiler_params=_cparams("parallel", "arbitrary", "arbitrary"),
        name="band_dil_attention",
    )(bias, h, h, h, h, h)


def _dil_out_ln_kernel(o0_ref, o1_ref, o2_ref, l0_ref, l1_ref, l2_ref, w_ref, x_ref, g_ref, b_ref, y_ref,
                       *, alpha):
    parts = []
    for p in range(PAIRS):
        l0, l1, l2 = l0_ref[0, p], l1_ref[0, p], l2_ref[0, p]
        m = jnp.maximum(jnp.maximum(l0, l1), l2)
        e0, e1, e2 = jnp.exp(l0 - m), jnp.exp(l1 - m), jnp.exp(l2 - m)
        merged = (e0 * o0_ref[0, p] + e1 * o1_ref[0, p] + e2 * o2_ref[0, p]) / (e0 + e1 + e2)
        parts.append(merged.astype(BF16))
    mix = _dot(jnp.concatenate(parts, axis=1), w_ref[...])
    y_ref[0] = _layernorm(alpha * x_ref[0] + mix, g_ref[...], b_ref[...])


def _dil_out_ln(outs, lses, w, layer, x, ln, alpha, tm=256):
    bsz, s, d = x.shape
    g, b, ln_row = ln
    slab = pl.BlockSpec((1, PAIRS, tm, LANES), lambda bi, i: (bi, 0, i, 0))
    row = pl.BlockSpec((1, tm, d), lambda bi, i: (bi, i, 0))
    return pl.pallas_call(
        functools.partial(_dil_out_ln_kernel, alpha=alpha),
        grid=(bsz, s // tm),
        in_specs=[slab] * 6 + [_layer_spec(w, layer), row, _layer_spec(g, ln_row), _layer_spec(b, ln_row)],
        out_specs=row,
        out_shape=jax.ShapeDtypeStruct((bsz, s, d), F32),
        compiler_params=_cparams("parallel", "parallel"),
        name="dil_out_ln",
    )(*outs, *lses, w, x, g, b)


PERM_QUAD = 0
PERM_KDUP = 1
PERM_VDUP = 1 + SWA_KV_HEADS


def _selection_matrices():
    sel = np.zeros((1 + 2 * SWA_KV_HEADS, QUAD_W, QUAD_W), np.float32)
    dims = np.arange(HALF_DIM)
    for half in range(2):
        for hq in range(QUAD):
            sel[PERM_QUAD, hq * HEAD_DIM + half * HALF_DIM + dims, half * LANES + hq * HALF_DIM + dims] = 1.0
            for kv in range(SWA_KV_HEADS):
                sel[PERM_KDUP + kv, kv * HEAD_DIM + half * HALF_DIM + dims, half * LANES + hq * HALF_DIM + dims] = 1.0
    dims = np.arange(HEAD_DIM)
    for kv in range(SWA_KV_HEADS):
        for rep in range(QUAD):
            sel[PERM_VDUP + kv, SWA_KV_HEADS * HEAD_DIM + kv * HEAD_DIM + dims, rep * HEAD_DIM + dims] = 1.0
    return jnp.asarray(sel, BF16)


def _prep_kernel(w_ref, sel_ref, o_ref, *, plan):
    for j, (src, perm, scale) in enumerate(plan):
        blk = w_ref[:, src:src + QUAD_W]
        if scale != 1.0:
            blk = blk * scale
        blk = blk.astype(BF16)
        if perm is not None:
            blk = _dot(blk, sel_ref[perm]).astype(BF16)
        o_ref[:, j * QUAD_W:(j + 1) * QUAD_W] = blk


def _prep_w(w, layer, col_block, width, plan, sel, rows=256):
    d = w.shape[1]
    return pl.pallas_call(
        functools.partial(_prep_kernel, plan=tuple(plan)),
        grid=(d // rows,),
        in_specs=[pl.BlockSpec((None, rows, width), lambda i: (layer, i, col_block)), _const_spec(sel.shape)],
        out_specs=pl.BlockSpec((rows, len(plan) * QUAD_W), lambda i: (i, 0)),
        out_shape=jax.ShapeDtypeStruct((d, len(plan) * QUAD_W), BF16),
        compiler_params=_cparams("parallel"),
        name="prep_w",
    )(w, sel)


def _rope_tables(seq):
    pos = jnp.arange(seq, dtype=F32)
    inv = ROPE_THETA ** (-jnp.arange(0, HEAD_DIM, 2, dtype=F32) / HEAD_DIM)
    ang = pos[:, None] * jnp.tile(inv, LANES // HALF_DIM)[None, :]
    return jnp.cos(ang), jnp.sin(ang)


def kernel(x, p, ffn1_w_in, ffn1_w_out, ffn2_w_in, ffn2_w_out, ln_g, ln_b, sb_w_in, sb_w_out, swa_w_in,
           swa_sinks, swa_w_out, dil_w_in, dil_w_out, ple_w_proj, ple_w_gate):
    bsz, seq, d = x.shape
    depth = p.shape[0]
    t = bsz * seq
    alpha = (2 * depth) ** 0.25
    q_scale = HEAD_DIM ** -0.5 * LOG2E
    width = HEADS * HEAD_DIM
    tables = _rope_tables(seq)
    sel = _selection_matrices()
    ffn_w = (_cast_bf16(ffn1_w_in, n_layers=1)[0], _cast_bf16(ffn1_w_out, 0.5, n_layers=1)[0])
    ple_gate, ple_proj = _cast_bf16(ple_w_gate), _cast_bf16(ple_w_proj)
    sb_wo, swa_wo, dil_wo = _cast_bf16(sb_w_out), _cast_bf16(swa_w_out), _cast_bf16(dil_w_out)
    p = p.reshape(depth, t, PLE_DIM)
    n_ln = ln_g.shape[1]
    ln_g = ln_g.reshape(depth * n_ln, 1, d)
    ln_b = ln_b.reshape(depth * n_ln, 1, d)
    x = x.reshape(t, d)
    for i in range(depth):
        kind, j = i % N_MIXERS, i // N_MIXERS
        ln = lambda k: (ln_g, ln_b, i * n_ln + k)
        x, ffn_w = _ffn(x, *ffn_w, ln(0), alpha, nxt=(ffn2_w_in, ffn2_w_out, i))
        x3d = x.reshape(bsz, seq, d)
        if kind == 0:
            plan = [(QUAD_W * k, None, q_scale if k < N_QUADS else 1.0) for k in range(3 * N_QUADS)]
            w = _prep_w(sb_w_in, j, 0, 3 * width, plan, sel)
            h = _proj(x3d, w, 1, (False,) * (3 * N_QUADS))
            mix = (_sb_attention(h).reshape(t, width), sb_wo, j, ln(1))
        elif kind == 1:
            plan = ([(QUAD_W * k, PERM_QUAD, q_scale) for k in range(N_QUADS)]
                    + [(width, PERM_KDUP + kv, 1.0) for kv in range(SWA_KV_HEADS)]
                    + [(width, PERM_VDUP + kv, 1.0) for kv in range(SWA_KV_HEADS)])
            w = _prep_w(swa_w_in, j, 0, swa_w_in.shape[2], plan, sel)
            roped = (True,) * (N_QUADS + SWA_KV_HEADS) + (False,) * SWA_KV_HEADS
            h = _proj(x3d, w, 1, roped, tables)
            o = _band_attention(h, SWA_WINDOW - 1, SWA_KV_HEADS * QUAD_W, lambda g: g // (N_QUADS // SWA_KV_HEADS),
                                swa_sinks[j])
            mix = (o.reshape(t, width), swa_wo, j, ln(1))
        else:
            outs, lses = [], []
            for gi, (win, dil) in enumerate(DIL_GROUPS):
                plan = ([(QUAD_W * k, PERM_QUAD, q_scale) for k in range(N_QUADS)]
                        + [(QUAD_W * k, PERM_QUAD, 1.0) for k in range(N_QUADS, 2 * N_QUADS)]
                        + [(QUAD_W * k, None, 1.0) for k in range(2 * N_QUADS, 3 * N_QUADS)])
                w = _prep_w(dil_w_in, j, gi, 3 * width, plan, sel)
                h = _proj(x3d, w, dil, (True,) * (2 * N_QUADS) + (False,) * N_QUADS, tables)
                o, lse = _band_dil_attention(h, win // dil)
                outs.append(o)
                lses.append(lse)
            x = _dil_out_ln(outs, lses, dil_wo, j, x3d, ln(1), alpha).reshape(t, d)
            mix = None
        nxt = (ffn1_w_in, ffn1_w_out, i + 1) if i + 1 < depth else None
        x, ffn_w = _ffn(x, *ffn_w, ln(2), alpha, ple=(p, ple_gate, ple_proj, i), mix=mix, nxt=nxt)
    return x.reshape(bsz, seq, d)
```

```python
import functools

import jax
import jax.numpy as jnp
import numpy as np
from jax import lax
from jax.experimental import pallas as pl
from jax.experimental.pallas import tpu as pltpu

F32 = jnp.float32
BF16 = jnp.bfloat16

D_MODEL = 1024
D_FF = 2816
PLE_DIM = 256
LN_EPS = 1e-5
ROPE_THETA = 10000.0
N_MIXERS = 3
HEADS = 16
HEAD_DIM = 64
HALF_DIM = HEAD_DIM // 2
SWA_KV_HEADS = 2
SWA_WINDOW = 128
DIL_GROUPS = ((128, 1), (512, 4), (2048, 16))
BLK = 128
NEG_INF = -1e30
LOG2E = 1.4426950408889634
LN2 = 0.6931471805599453
SB_DEAD_MASS = 110.0 * LOG2E

LANES = 128
PAIRS = HEADS // 2
QUAD = 4
QUAD_W = QUAD * HEAD_DIM
N_QUADS = HEADS // QUAD
FF_CHUNK = 256
N_FF_CHUNKS = D_FF // FF_CHUNK
PROJ_CHUNK = 2 * QUAD_W
VMEM_LIMIT = 56 * 1024 * 1024


def _cparams(*sem):
    return pltpu.CompilerParams(dimension_semantics=sem, vmem_limit_bytes=VMEM_LIMIT)


def _dot(a, b):
    return jnp.dot(a, b, preferred_element_type=F32)


def _dot_nt(a, b):
    return lax.dot_general(a, b, (((1,), (1,)), ((), ())), preferred_element_type=F32)


def _layernorm(y, g, b):
    mu = jnp.mean(y, axis=-1, keepdims=True)
    yc = y - mu
    var = jnp.mean(yc * yc, axis=-1, keepdims=True)
    return yc * lax.rsqrt(var + LN_EPS) * g + b


def _sigmoid(x):
    return 1.0 / (1.0 + jnp.exp(-x))


def _const_spec(shape):
    nd = len(shape)
    return pl.BlockSpec(shape, lambda *_: (0,) * nd, pipeline_mode=pl.Buffered(1))


def _layer_spec(stack, layer):
    return pl.BlockSpec((None,) + stack.shape[1:], lambda *_: (layer, 0, 0), pipeline_mode=pl.Buffered(1))


def _cast_kernel(w_ref, o_ref, *, scale):
    w = w_ref[...]
    if scale != 1.0:
        w = w * scale
    o_ref[...] = w.astype(o_ref.dtype)


def _cast_bf16(w, scale=1.0, rows=256, n_layers=None):
    _, r, c = w.shape
    n_layers = w.shape[0] if n_layers is None else n_layers
    rb = min(rows, r)
    spec = pl.BlockSpec((1, rb, c), lambda a, b: (a, b, 0))
    return pl.pallas_call(
        functools.partial(_cast_kernel, scale=scale),
        grid=(n_layers, r // rb),
        in_specs=[spec],
        out_specs=spec,
        out_shape=jax.ShapeDtypeStruct((n_layers, r, c), BF16),
        compiler_params=_cparams("parallel", "parallel"),
        name="cast_bf16",
    )(w)


def _ffn_kernel(*refs, alpha, has_mix, has_ple, has_next):
    refs = list(refs)
    take = lambda n: [refs.pop(0) for _ in range(n)]
    if has_mix:
        a_ref, wm_ref, gm_ref, bm_ref = take(4)
    x_ref, wi_ref, wo_ref, g_ref, b_ref = take(5)
    if has_ple:
        p_ref, pg_ref, pp_ref = take(3)
    if has_next:
        nwi_ref, nwo_ref = take(2)
    (o_ref,) = take(1)
    if has_next:
        nwi_out, nwo_out = take(2)
        nwi_out[...] = nwi_ref[...].astype(BF16)
        nwo_out[...] = (0.5 * nwo_ref[...]).astype(BF16)
    (h_ref,) = take(1)

    if has_mix:
        mixed = alpha * x_ref[...] + _dot(a_ref[...], wm_ref[...])
        o_ref[...] = _layernorm(mixed, gm_ref[...], bm_ref[...])
        x_ref = o_ref
    xb = x_ref[...].astype(BF16)
    for c in range(N_FF_CHUNKS):
        lo = c * FF_CHUNK
        gate = _dot(xb, wi_ref[:, lo:lo + FF_CHUNK])
        up = _dot(xb, wi_ref[:, D_FF + lo:D_FF + lo + FF_CHUNK])
        h_ref[:, lo:lo + FF_CHUNK] = ((gate * _sigmoid(gate)) * up).astype(BF16)
    y = _layernorm(alpha * x_ref[...] + _dot(h_ref[...], wo_ref[...]), g_ref[...], b_ref[...])
    if has_ple:
        gate = _dot(y.astype(BF16), pg_ref[...])
        proj = _dot(p_ref[...].astype(BF16), pp_ref[...])
        y = y + _sigmoid(gate) * proj
    o_ref[...] = y


def _ffn(x, wi, wo, ln, alpha, ple=None, mix=None, nxt=None, tm=512):
    t = x.shape[0]
    steps = t // tm
    row = pl.BlockSpec((tm, D_MODEL), lambda i: (i, 0))
    g, b, ln_row = ln
    in_specs = [row, _const_spec(wi.shape), _const_spec(wo.shape), _layer_spec(g, ln_row), _layer_spec(b, ln_row)]
    args = [x, wi, wo, g, b]
    out_specs, out_shape = [row], [jax.ShapeDtypeStruct((t, D_MODEL), F32)]
    if mix is not None:
        a, wm, mix_layer, (gm, bm, mix_row) = mix
        in_specs = [row, _layer_spec(wm, mix_layer), _layer_spec(gm, mix_row), _layer_spec(bm, mix_row)] + in_specs
        args = [a, wm, gm, bm] + args
    if ple is not None:
        p, pg, pp, layer = ple
        in_specs += [pl.BlockSpec((None, tm, PLE_DIM), lambda i: (layer, i, 0)), _layer_spec(pg, layer),
                     _layer_spec(pp, layer)]
        args += [p, pg, pp]
    if nxt is not None:
        nwi, nwo, nlayer = nxt
        ri = nwi.shape[1] // steps
        so = steps // 2
        ro = nwo.shape[1] // so
        in_specs += [pl.BlockSpec((None, ri, nwi.shape[2]), lambda i: (nlayer, i, 0)),
                     pl.BlockSpec((None, ro, nwo.shape[2]), lambda i: (nlayer, jnp.minimum(i, so - 1), 0))]
        args += [nwi, nwo]
        out_specs += [pl.BlockSpec((ri, nwi.shape[2]), lambda i: (i, 0)),
                      pl.BlockSpec((ro, nwo.shape[2]), lambda i: (jnp.minimum(i, so - 1), 0))]
        out_shape += [jax.ShapeDtypeStruct(nwi.shape[1:], BF16), jax.ShapeDtypeStruct(nwo.shape[1:], BF16)]
    res = pl.pallas_call(
        functools.partial(_ffn_kernel, alpha=alpha, has_mix=mix is not None, has_ple=ple is not None,
                          has_next=nxt is not None),
        grid=(steps,),
        in_specs=in_specs,
        out_specs=out_specs,
        out_shape=out_shape,
        scratch_shapes=[pltpu.VMEM((tm, D_FF), BF16)],
        compiler_params=_cparams("arbitrary"),
        name="ffn",
    )(*args)
    return res[0], tuple(res[1:])


def _proj_kernel(*refs, roped, dil):
    refs = list(refs)
    slab_ref = refs.pop() if dil > 1 else None
    if any(roped):
        x_ref, w_ref, cos_ref, sin_ref, o_ref = refs
    else:
        x_ref, w_ref, o_ref = refs
    tm = x_ref.shape[1]
    rows = tm // dil

    def by_class(ref128):
        return jnp.concatenate([ref128[pl.ds(r, rows, stride=dil), :] for r in range(dil)], axis=0)

    if dil > 1:
        n_slabs = x_ref.shape[2] // LANES
        for s in range(n_slabs):
            slab_ref[s] = x_ref[0, :, s * LANES:(s + 1) * LANES]
        x = jnp.concatenate([by_class(slab_ref.at[s]) for s in range(n_slabs)], axis=1)
    else:
        x = x_ref[0]
    if any(roped):
        cos, sin = (by_class(cos_ref), by_class(sin_ref)) if dil > 1 else (cos_ref[...], sin_ref[...])

    def store(col, val):
        for r in range(dil):
            o_ref[0, r, :, col:col + val.shape[1]] = val[r * rows:(r + 1) * rows].astype(o_ref.dtype)

    xb = x.astype(BF16)
    per_chunk = PROJ_CHUNK // QUAD_W
    for c in range(len(roped) // per_chunk):
        acc = _dot(xb, w_ref[:, c * PROJ_CHUNK:(c + 1) * PROJ_CHUNK])
        for k in range(per_chunk):
            lo = k * QUAD_W
            col = c * PROJ_CHUNK + lo
            if roped[c * per_chunk + k]:
                x1, x2 = acc[:, lo:lo + LANES], acc[:, lo + LANES:lo + QUAD_W]
                store(col, x1 * cos - x2 * sin)
                store(col + LANES, x2 * cos + x1 * sin)
            else:
                store(col, acc[:, lo:lo + QUAD_W])


def _proj(x, w, dil, roped, tables=None, tm=512):
    bsz, s, d = x.shape
    n = w.shape[1]
    in_specs = [pl.BlockSpec((1, tm, d), lambda b, i: (b, i, 0)), _const_spec(w.shape)]
    args = [x, w]
    if any(roped):
        tab_spec = pl.BlockSpec((tm, LANES), lambda b, i: (i, 0))
        in_specs += [tab_spec, tab_spec]
        args += list(tables)
    return pl.pallas_call(
        functools.partial(_proj_kernel, roped=tuple(roped), dil=dil),
        grid=(bsz, s // tm),
        in_specs=in_specs,
        out_specs=pl.BlockSpec((1, dil, tm // dil, n), lambda b, i: (b, 0, i, 0)),
        out_shape=jax.ShapeDtypeStruct((bsz, dil, s // dil, n), BF16),
        scratch_shapes=[pltpu.VMEM((d // LANES, tm, LANES), F32)] if dil > 1 else [],
        compiler_params=_cparams("parallel", "parallel"),
        name="mixer_proj",
    )(*args)


def _sb_kernel(q_ref, k_ref, v_ref, o_ref, acc_ref, r_ref, *, tq):
    i = pl.program_id(1)
    has_prev = i > 0
    d_start = pl.multiple_of(i * tq, tq)
    p_start = pl.multiple_of(jnp.maximum(i - 1, 0) * tq, tq)
    in_a = lax.broadcasted_iota(jnp.int32, (1, LANES), 1) < HEAD_DIM
    row = lax.broadcasted_iota(jnp.int32, (tq, tq), 0)
    col = lax.broadcasted_iota(jnp.int32, (tq, tq), 1)
    suffix = (row >= col).astype(BF16)
    row2 = lax.broadcasted_iota(jnp.int32, (2 * tq, tq), 0) % tq
    col2 = lax.broadcasted_iota(jnp.int32, (2 * tq, tq), 1)
    strict = col2 < row2

    def stacked_q(sl):
        q2 = q_ref[0, 0, :, sl]
        zero = jnp.zeros_like(q2)
        return jnp.concatenate([jnp.where(in_a, q2, zero), jnp.where(in_a, zero, q2)], axis=0)

    def softplus(z):
        return jnp.maximum(z, 0.0) + jnp.log2(1.0 + jnp.exp2(-jnp.abs(z)))

    def fold(pv):
        return jnp.where(in_a, pv[:tq], pv[tq:])

    for p in range(PAIRS):
        sl = slice(p * LANES, (p + 1) * LANES)
        qs = stacked_q(sl)
        z = _dot_nt(qs, k_ref[0, 0, pl.ds(d_start, tq), sl])
        sp = jnp.where(strict, softplus(z), 0.0)
        a = jnp.where(strict, jnp.exp2(z - _dot(sp.astype(BF16), suffix)), 0.0)
        pv = _dot(a.astype(BF16), v_ref[0, 0, pl.ds(d_start, tq), sl])
        r = jnp.sum(sp, axis=1, keepdims=True)
        z = _dot_nt(qs, k_ref[0, 0, pl.ds(p_start, tq), sl])
        sp = softplus(z)
        a = jnp.where(has_prev, jnp.exp2(z - _dot(sp.astype(BF16), suffix) - r), 0.0)
        pv = pv + _dot(a.astype(BF16), v_ref[0, 0, pl.ds(p_start, tq), sl])
        acc_ref[:, sl] = fold(pv)
        r_ref[p] = r + jnp.where(has_prev, jnp.sum(sp, axis=1, keepdims=True), 0.0)

    def more(state):
        j, mass = state[0], state[1]
        return jnp.logical_and(j >= 0, mass < SB_DEAD_MASS)

    for p in range(PAIRS):
        sl = slice(p * LANES, (p + 1) * LANES)

        def step(state, sl=sl):
            j, _, acc, r = state
            start = pl.multiple_of(j * tq, tq)
            z = _dot_nt(stacked_q(sl), k_ref[0, 0, pl.ds(start, tq), sl])
            sp = softplus(z)
            a = jnp.exp2(z - _dot(sp.astype(BF16), suffix) - r)
            acc = acc + fold(_dot(a.astype(BF16), v_ref[0, 0, pl.ds(start, tq), sl]))
            r = r + jnp.sum(sp, axis=1, keepdims=True)
            return (j - 1, jnp.min(r), acc, r)

        r = r_ref[p]
        state = lax.while_loop(more, step, (i - 2, jnp.min(r), acc_ref[:, sl], r))
        o_ref[0, :, sl] = state[2].astype(o_ref.dtype)


def _sb_attention(h, tq=256):
    bsz, _, s, _ = h.shape
    whole = lambda blk: pl.BlockSpec((1, 1, s, D_MODEL), lambda b, i: (b, 0, 0, blk), pipeline_mode=pl.Buffered(1))
    return pl.pallas_call(
        functools.partial(_sb_kernel, tq=tq),
        grid=(bsz, s // tq),
        in_specs=[pl.BlockSpec((1, 1, tq, D_MODEL), lambda b, i: (b, 0, i, 0)), whole(1), whole(2)],
        out_specs=pl.BlockSpec((1, tq, D_MODEL), lambda b, i: (b, i, 0)),
        out_shape=jax.ShapeDtypeStruct((bsz, s, D_MODEL), BF16),
        scratch_shapes=[pltpu.VMEM((tq, D_MODEL), F32), pltpu.VMEM((PAIRS, 2 * tq, 1), F32)],
        compiler_params=_cparams("parallel", "arbitrary"),
        name="sb_attention",
    )(h, h, h)


def _band_bias(max_dist):
    row = lax.broadcasted_iota(jnp.int32, (QUAD * BLK, 2 * BLK), 0)
    kj = lax.broadcasted_iota(jnp.int32, (QUAD * BLK, 2 * BLK), 1)
    diff = BLK + row % BLK - kj
    band = (diff >= 0) & (diff <= max_dist)
    return jnp.where(jnp.stack([band & (kj >= BLK), band]), 0.0, NEG_INF).astype(F32)


def _band_lane_masks():
    lane = lax.broadcasted_iota(jnp.int32, (1, QUAD_W), 1)
    qk_sel = [(lane % LANES) // HALF_DIM == hq for hq in range(QUAD)]
    in_a = lax.broadcasted_iota(jnp.int32, (1, LANES), 1) < HEAD_DIM
    return qk_sel, in_a


def _band_quad(q4, kk, bias, qk_sel, sink=None):
    zero = jnp.zeros_like(q4)
    qs = jnp.concatenate([jnp.where(qk_sel[hq], q4, zero) for hq in range(QUAD)], axis=0)
    s = _dot_nt(qs, kk) + bias
    m = jnp.max(s, axis=1, keepdims=True)
    if sink is not None:
        m = jnp.maximum(m, sink)
    e = jnp.exp2(s - m)
    den = jnp.sum(e, axis=1, keepdims=True)
    if sink is not None:
        den = den + jnp.exp2(sink - m)
    return e.astype(BF16), den, m * LN2 + jnp.log(den)


def _band_kernel(sink_ref, bias_ref, q_ref, kp_ref, kc_ref, vp_ref, vc_ref, o_ref, *, kv_quad):
    qk_sel, in_a = _band_lane_masks()
    which = jnp.minimum(pl.program_id(1), 1)
    head_of_row = lax.broadcasted_iota(jnp.int32, (QUAD * BLK, 1), 0) // BLK
    for g in range(N_QUADS):
        c0 = kv_quad(g) * QUAD_W
        kk = jnp.concatenate([kp_ref[0, 0, :, c0:c0 + QUAD_W], kc_ref[0, 0, :, c0:c0 + QUAD_W]], axis=0)
        sink = jnp.full((QUAD * BLK, 1), sink_ref[QUAD * g] * LOG2E, F32)
        for hq in range(1, QUAD):
            sink = jnp.where(head_of_row == hq, sink_ref[QUAD * g + hq] * LOG2E, sink)
        eb, den, _ = _band_quad(q_ref[0, 0, :, g * QUAD_W:(g + 1) * QUAD_W], kk, bias_ref[which], qk_sel, sink)
        for pr in range(QUAD // 2):
            c1 = c0 + pr * LANES
            rows = slice(2 * pr * BLK, 2 * (pr + 1) * BLK)
            vv = jnp.concatenate([vp_ref[0, 0, :, c1:c1 + LANES], vc_ref[0, 0, :, c1:c1 + LANES]], axis=0)
            o2 = _dot(eb[rows], vv) / den[rows]
            col = g * QUAD_W + pr * LANES
            o_ref[0, :, col:col + LANES] = jnp.where(in_a, o2[:BLK], o2[BLK:]).astype(o_ref.dtype)


def _band_attention(h, max_dist, kv_width, kv_quad, sinks):
    bsz, _, s, _ = h.shape
    k_blk = D_MODEL // kv_width
    bias = _band_bias(max_dist)
    cur = lambda blk: (lambda b, n: (b, 0, n, blk))
    prev = lambda blk: (lambda b, n: (b, 0, jnp.maximum(n - 1, 0), blk))
    kv_spec = lambda imap: pl.BlockSpec((1, 1, BLK, kv_width), imap)
    return pl.pallas_call(
        functools.partial(_band_kernel, kv_quad=kv_quad),
        grid=(bsz, s // BLK),
        in_specs=[pl.BlockSpec(memory_space=pltpu.SMEM), _const_spec(bias.shape), pl.BlockSpec((1, 1, BLK, D_MODEL), cur(0)),
                  kv_spec(prev(k_blk)), kv_spec(cur(k_blk)), kv_spec(prev(k_blk + 1)), kv_spec(cur(k_blk + 1))],
        out_specs=pl.BlockSpec((1, BLK, D_MODEL), lambda b, n: (b, n, 0)),
        out_shape=jax.ShapeDtypeStruct((bsz, s, D_MODEL), BF16),
        compiler_params=_cparams("parallel", "arbitrary"),
        name="band_attention",
    )(sinks, bias, h, h, h, h, h)


def _band_dil_kernel(bias_ref, q_ref, kp_ref, kc_ref, vp_ref, vc_ref, o_ref, lse_ref, *, dil, quads):
    qk_sel, in_a = _band_lane_masks()
    which = jnp.minimum(pl.program_id(1), 1)

    head_lane = lax.broadcasted_iota(jnp.int32, (1, LANES), 1)
    first_head = pl.program_id(2) * (quads * QUAD)

    def one_class(r):
        dst = pl.ds(r, BLK, stride=dil) if dil > 1 else slice(None)
        lse_tile = jnp.zeros((BLK, LANES), F32)
        for g in range(quads):
            cols = slice(g * QUAD_W, (g + 1) * QUAD_W)
            kk = jnp.concatenate([kp_ref[0, r, :, cols], kc_ref[0, r, :, cols]], axis=0)
            eb, den, lse = _band_quad(q_ref[0, r, :, cols], kk, bias_ref[which], qk_sel)
            for hq in range(QUAD):
                lse_tile = jnp.where(head_lane == first_head + (g * QUAD + hq), lse[hq * BLK:(hq + 1) * BLK],
                                     lse_tile)
            for pr in range(QUAD // 2):
                c1 = slice(g * QUAD_W + pr * LANES, g * QUAD_W + (pr + 1) * LANES)
                rows = slice(2 * pr * BLK, 2 * (pr + 1) * BLK)
                vv = jnp.concatenate([vp_ref[0, r, :, c1], vc_ref[0, r, :, c1]], axis=0)
                o2 = _dot(eb[rows], vv) / den[rows]
                o_ref[0, 2 * g + pr, dst, :] = jnp.where(in_a, o2[:BLK], o2[BLK:])
        lse_ref[0, 0, dst, :] = lse_tile

    if dil == 1:
        one_class(0)
    else:
        def body(r2, carry):
            one_class(2 * r2)
            one_class(2 * r2 + 1)
            return carry
        lax.fori_loop(0, dil // 2, body, 0)


def _band_dil_attention(h, max_dist):
    bsz, dil, ls, _ = h.shape
    quads = N_QUADS if dil <= QUAD else N_QUADS // 2
    width = quads * QUAD_W
    nq = D_MODEL // width
    bias = _band_bias(max_dist)
    cur = lambda base: (lambda b, n, g: (b, 0, n, base + g))
    prev = lambda base: (lambda b, n, g: (b, 0, jnp.maximum(n - 1, 0), base + g))
    spec = lambda imap: pl.BlockSpec((1, dil, BLK, width), imap)
    o_spec = pl.BlockSpec((1, 2 * quads, BLK * dil, LANES), lambda b, n, g: (b, g, n, 0))
    o_shape = jax.ShapeDtypeStruct((bsz, PAIRS, ls * dil, LANES), F32)
    return pl.pallas_call(
        functools.partial(_band_dil_kernel, dil=dil, quads=quads),
        grid=(bsz, ls // BLK, nq),
        in_specs=[_const_spec(bias.shape), spec(cur(0)), spec(prev(nq)), spec(cur(nq)), spec(prev(2 * nq)),
                  spec(cur(2 * nq))],
        out_specs=[o_spec, pl.BlockSpec((1, 1, BLK * dil, LANES), lambda b, n, g: (b, g, n, 0))],
        out_shape=[o_shape, jax.ShapeDtypeStruct((bsz, nq, ls * dil, LANES), F32)],
        compiler_params=_cparams("parallel", "arbitrary", "arbitrary"),
        name="band_dil_attention",
    )(bias, h, h, h, h, h)


def _dil_out_ln_kernel(o0_ref, o1_ref, o2_ref, l0_ref, l1_ref, l2_ref, spread_ref, w_ref, x_ref, g_ref, b_ref,
                       y_ref, *, alpha):
    def per_head(l_ref):
        lc = l_ref[0, 0]
        for k in range(1, l_ref.shape[1]):
            lc = lc + l_ref[0, k]
        return lc

    def spread(wc):
        hi = wc.astype(BF16)
        lo = (wc - hi.astype(F32)).astype(BF16)
        return _dot(hi, spread_ref[...]) + _dot(lo, spread_ref[...])

    l0, l1, l2 = per_head(l0_ref), per_head(l1_ref), per_head(l2_ref)
    m = jnp.maximum(jnp.maximum(l0, l1), l2)
    e0, e1, e2 = jnp.exp(l0 - m), jnp.exp(l1 - m), jnp.exp(l2 - m)
    inv = 1.0 / (e0 + e1 + e2)
    w0, w1, w2 = spread(e0 * inv), spread(e1 * inv), spread(e2 * inv)
    parts = []
    for p in range(PAIRS):
        sl = slice(p * LANES, (p + 1) * LANES)
        merged = w0[:, sl] * o0_ref[0, p] + w1[:, sl] * o1_ref[0, p] + w2[:, sl] * o2_ref[0, p]
        parts.append(merged.astype(BF16))
    mix = _dot(jnp.concatenate(parts, axis=1), w_ref[...])
    y_ref[0] = _layernorm(alpha * x_ref[0] + mix, g_ref[...], b_ref[...])


def _dil_out_ln(outs, lses, w, layer, x, ln, alpha, tm=256):
    bsz, s, d = x.shape
    g, b, ln_row = ln
    slab = pl.BlockSpec((1, PAIRS, tm, LANES), lambda bi, i: (bi, 0, i, 0))
    lse_specs = [pl.BlockSpec((1, l.shape[1], tm, LANES), lambda bi, i: (bi, 0, i, 0)) for l in lses]
    spread = (np.arange(LANES)[:, None] == np.arange(d)[None, :] // HEAD_DIM).astype(np.float32)
    spread = jnp.asarray(spread, BF16)
    row = pl.BlockSpec((1, tm, d), lambda bi, i: (bi, i, 0))
    return pl.pallas_call(
        functools.partial(_dil_out_ln_kernel, alpha=alpha),
        grid=(bsz, s // tm),
        in_specs=[slab] * 3 + lse_specs + [_const_spec(spread.shape), _layer_spec(w, layer), row,
                  _layer_spec(g, ln_row), _layer_spec(b, ln_row)],
        out_specs=row,
        out_shape=jax.ShapeDtypeStruct((bsz, s, d), F32),
        compiler_params=_cparams("parallel", "parallel"),
        name="dil_out_ln",
    )(*outs, *lses, spread, w, x, g, b)


PERM_QUAD = 0
PERM_KDUP = 1
PERM_VDUP = 1 + SWA_KV_HEADS


def _selection_matrices():
    sel = np.zeros((1 + 2 * SWA_KV_HEADS, QUAD_W, QUAD_W), np.float32)
    dims = np.arange(HALF_DIM)
    for half in range(2):
        for hq in range(QUAD):
            sel[PERM_QUAD, hq * HEAD_DIM + half * HALF_DIM + dims, half * LANES + hq * HALF_DIM + dims] = 1.0
            for kv in range(SWA_KV_HEADS):
                sel[PERM_KDUP + kv, kv * HEAD_DIM + half * HALF_DIM + dims, half * LANES + hq * HALF_DIM + dims] = 1.0
    dims = np.arange(HEAD_DIM)
    for kv in range(SWA_KV_HEADS):
        for rep in range(QUAD):
            sel[PERM_VDUP + kv, SWA_KV_HEADS * HEAD_DIM + kv * HEAD_DIM + dims, rep * HEAD_DIM + dims] = 1.0
    return jnp.asarray(sel, BF16)


def _prep_kernel(w_ref, sel_ref, o_ref, *, plan):
    for j, (src, perm, scale) in enumerate(plan):
        blk = w_ref[:, src:src + QUAD_W]
        if scale != 1.0:
            blk = blk * scale
        blk = blk.astype(BF16)
        if perm is not None:
            blk = _dot(blk, sel_ref[perm]).astype(BF16)
        o_ref[:, j * QUAD_W:(j + 1) * QUAD_W] = blk


def _prep_w(w, layer, col_block, width, plan, sel, rows=256):
    d = w.shape[1]
    return pl.pallas_call(
        functools.partial(_prep_kernel, plan=tuple(plan)),
        grid=(d // rows,),
        in_specs=[pl.BlockSpec((None, rows, width), lambda i: (layer, i, col_block)), _const_spec(sel.shape)],
        out_specs=pl.BlockSpec((rows, len(plan) * QUAD_W), lambda i: (i, 0)),
        out_shape=jax.ShapeDtypeStruct((d, len(plan) * QUAD_W), BF16),
        compiler_params=_cparams("parallel"),
        name="prep_w",
    )(w, sel)


def _rope_tables(seq):
    pos = jnp.arange(seq, dtype=F32)
    inv = ROPE_THETA ** (-jnp.arange(0, HEAD_DIM, 2, dtype=F32) / HEAD_DIM)
    ang = pos[:, None] * jnp.tile(inv, LANES // HALF_DIM)[None, :]
    return jnp.cos(ang), jnp.sin(ang)


def kernel(x, p, ffn1_w_in, ffn1_w_out, ffn2_w_in, ffn2_w_out, ln_g, ln_b, sb_w_in, sb_w_out, swa_w_in,
           swa_sinks, swa_w_out, dil_w_in, dil_w_out, ple_w_proj, ple_w_gate):
    bsz, seq, d = x.shape
    depth = p.shape[0]
    t = bsz * seq
    alpha = (2 * depth) ** 0.25
    q_scale = HEAD_DIM ** -0.5 * LOG2E
    width = HEADS * HEAD_DIM
    tables = _rope_tables(seq)
    sel = _selection_matrices()
    ffn_w = (_cast_bf16(ffn1_w_in, n_layers=1)[0], _cast_bf16(ffn1_w_out, 0.5, n_layers=1)[0])
    ple_gate, ple_proj = _cast_bf16(ple_w_gate), _cast_bf16(ple_w_proj)
    sb_wo, swa_wo, dil_wo = _cast_bf16(sb_w_out), _cast_bf16(swa_w_out), _cast_bf16(dil_w_out)
    p = p.reshape(depth, t, PLE_DIM)
    n_ln = ln_g.shape[1]
    ln_g = ln_g.reshape(depth * n_ln, 1, d)
    ln_b = ln_b.reshape(depth * n_ln, 1, d)
    x = x.reshape(t, d)
    for i in range(depth):
        kind, j = i % N_MIXERS, i // N_MIXERS
        ln = lambda k: (ln_g, ln_b, i * n_ln + k)
        x, ffn_w = _ffn(x, *ffn_w, ln(0), alpha, nxt=(ffn2_w_in, ffn2_w_out, i))
        x3d = x.reshape(bsz, seq, d)
        if kind == 0:
            plan = [(QUAD_W * k, None, q_scale if k < N_QUADS else 1.0) for k in range(3 * N_QUADS)]
            w = _prep_w(sb_w_in, j, 0, 3 * width, plan, sel)
            h = _proj(x3d, w, 1, (False,) * (3 * N_QUADS))
            mix = (_sb_attention(h).reshape(t, width), sb_wo, j, ln(1))
        elif kind == 1:
            plan = ([(QUAD_W * k, PERM_QUAD, q_scale) for k in range(N_QUADS)]
                    + [(width, PERM_KDUP + kv, 1.0) for kv in range(SWA_KV_HEADS)]
                    + [(width, PERM_VDUP + kv, 1.0) for kv in range(SWA_KV_HEADS)])
            w = _prep_w(swa_w_in, j, 0, swa_w_in.shape[2], plan, sel)
            roped = (True,) * (N_QUADS + SWA_KV_HEADS) + (False,) * SWA_KV_HEADS
            h = _proj(x3d, w, 1, roped, tables)
            o = _band_attention(h, SWA_WINDOW - 1, SWA_KV_HEADS * QUAD_W, lambda g: g // (N_QUADS // SWA_KV_HEADS),
                                swa_sinks[j])
            mix = (o.reshape(t, width), swa_wo, j, ln(1))
        else:
            outs, lses = [], []
            for gi, (win, dil) in enumerate(DIL_GROUPS):
                plan = ([(QUAD_W * k, PERM_QUAD, q_scale) for k in range(N_QUADS)]
                        + [(QUAD_W * k, PERM_QUAD, 1.0) for k in range(N_QUADS, 2 * N_QUADS)]
                        + [(QUAD_W * k, None, 1.0) for k in range(2 * N_QUADS, 3 * N_QUADS)])
                w = _prep_w(dil_w_in, j, gi, 3 * width, plan, sel)
                h = _proj(x3d, w, dil, (True,) * (2 * N_QUADS) + (False,) * N_QUADS, tables)
                o, lse = _band_dil_attention(h, win // dil)
                outs.append(o)
                lses.append(lse)
            x = _dil_out_ln(outs, lses, dil_wo, j, x3d, ln(1), alpha).reshape(t, d)
            mix = None
        nxt = (ffn1_w_in, ffn1_w_out, i + 1) if i + 1 < depth else None
        x, ffn_w = _ffn(x, *ffn_w, ln(2), alpha, ple=(p, ple_gate, ple_proj, i), mix=mix, nxt=nxt)
    return x.reshape(bsz, seq, d)
```

```python
import functools

import jax
import jax.numpy as jnp
import numpy as np
from jax import lax
from jax.experimental import pallas as pl
from jax.experimental.pallas import tpu as pltpu

F32 = jnp.float32
BF16 = jnp.bfloat16

D_MODEL = 1024
D_FF = 2816
PLE_DIM = 256
LN_EPS = 1e-5
ROPE_THETA = 10000.0
N_MIXERS = 3
HEADS = 16
HEAD_DIM = 64
HALF_DIM = HEAD_DIM // 2
SWA_KV_HEADS = 2
SWA_WINDOW = 128
DIL_GROUPS = ((128, 1), (512, 4), (2048, 16))
BLK = 128
NEG_INF = -1e30
LOG2E = 1.4426950408889634
LN2 = 0.6931471805599453
SB_DEAD_MASS = 110.0 * LOG2E

LANES = 128
PAIRS = HEADS // 2
QUAD = 4
QUAD_W = QUAD * HEAD_DIM
N_QUADS = HEADS // QUAD
FF_CHUNK = 256
N_FF_CHUNKS = D_FF // FF_CHUNK
PROJ_CHUNK = 2 * QUAD_W
VMEM_LIMIT = 56 * 1024 * 1024


def _cparams(*sem):
    return pltpu.CompilerParams(dimension_semantics=sem, vmem_limit_bytes=VMEM_LIMIT)


def _dot(a, b):
    return jnp.dot(a, b, preferred_element_type=F32)


def _dot_nt(a, b):
    return lax.dot_general(a, b, (((1,), (1,)), ((), ())), preferred_element_type=F32)


def _layernorm(y, g, b):
    mu = jnp.mean(y, axis=-1, keepdims=True)
    yc = y - mu
    var = jnp.mean(yc * yc, axis=-1, keepdims=True)
    return yc * lax.rsqrt(var + LN_EPS) * g + b


def _sigmoid(x):
    return 1.0 / (1.0 + jnp.exp(-x))


def _const_spec(shape):
    nd = len(shape)
    return pl.BlockSpec(shape, lambda *_: (0,) * nd, pipeline_mode=pl.Buffered(1))


def _layer_spec(stack, layer):
    return pl.BlockSpec((None,) + stack.shape[1:], lambda *_: (layer, 0, 0), pipeline_mode=pl.Buffered(1))


def _cast_kernel(w_ref, o_ref, *, scale):
    w = w_ref[...]
    if scale != 1.0:
        w = w * scale
    o_ref[...] = w.astype(o_ref.dtype)


def _cast_bf16(w, scale=1.0, rows=256, n_layers=None):
    _, r, c = w.shape
    n_layers = w.shape[0] if n_layers is None else n_layers
    rb = min(rows, r)
    spec = pl.BlockSpec((1, rb, c), lambda a, b: (a, b, 0))
    return pl.pallas_call(
        functools.partial(_cast_kernel, scale=scale),
        grid=(n_layers, r // rb),
        in_specs=[spec],
        out_specs=spec,
        out_shape=jax.ShapeDtypeStruct((n_layers, r, c), BF16),
        compiler_params=_cparams("parallel", "parallel"),
        name="cast_bf16",
    )(w)


def _ffn_kernel(*refs, alpha, has_mix, has_ple, has_next):
    refs = list(refs)
    take = lambda n: [refs.pop(0) for _ in range(n)]
    if has_mix:
        a_ref, wm_ref, gm_ref, bm_ref = take(4)
    x_ref, wi_ref, wo_ref, g_ref, b_ref = take(5)
    if has_ple:
        p_ref, pg_ref, pp_ref = take(3)
    if has_next:
        nwi_ref, nwo_ref = take(2)
    (o_ref,) = take(1)
    if has_next:
        nwi_out, nwo_out = take(2)
        nwi_out[...] = nwi_ref[...].astype(BF16)
        nwo_out[...] = (0.5 * nwo_ref[...]).astype(BF16)
    (h_ref,) = take(1)

    if has_mix:
        mixed = alpha * x_ref[...] + _dot(a_ref[...], wm_ref[...])
        o_ref[...] = _layernorm(mixed, gm_ref[...], bm_ref[...])
        x_ref = o_ref
    xb = x_ref[...].astype(BF16)
    for c in range(N_FF_CHUNKS):
        lo = c * FF_CHUNK
        gate = _dot(xb, wi_ref[:, lo:lo + FF_CHUNK])
        up = _dot(xb, wi_ref[:, D_FF + lo:D_FF + lo + FF_CHUNK])
        h_ref[:, lo:lo + FF_CHUNK] = ((gate * _sigmoid(gate)) * up).astype(BF16)
    y = _layernorm(alpha * x_ref[...] + _dot(h_ref[...], wo_ref[...]), g_ref[...], b_ref[...])
    if has_ple:
        gate = _dot(y.astype(BF16), pg_ref[...])
        proj = _dot(p_ref[...].astype(BF16), pp_ref[...])
        y = y + _sigmoid(gate) * proj
    o_ref[...] = y


def _ffn(x, wi, wo, ln, alpha, ple=None, mix=None, nxt=None, tm=512):
    t = x.shape[0]
    steps = t // tm
    row = pl.BlockSpec((tm, D_MODEL), lambda i: (i, 0))
    g, b, ln_row = ln
    in_specs = [row, _const_spec(wi.shape), _const_spec(wo.shape), _layer_spec(g, ln_row), _layer_spec(b, ln_row)]
    args = [x, wi, wo, g, b]
    out_specs, out_shape = [row], [jax.ShapeDtypeStruct((t, D_MODEL), F32)]
    if mix is not None:
        a, wm, mix_layer, (gm, bm, mix_row) = mix
        in_specs = [row, _layer_spec(wm, mix_layer), _layer_spec(gm, mix_row), _layer_spec(bm, mix_row)] + in_specs
        args = [a, wm, gm, bm] + args
    if ple is not None:
        p, pg, pp, layer = ple
        in_specs += [pl.BlockSpec((None, tm, PLE_DIM), lambda i: (layer, i, 0)), _layer_spec(pg, layer),
                     _layer_spec(pp, layer)]
        args += [p, pg, pp]
    if nxt is not None:
        nwi, nwo, nlayer = nxt
        ri = nwi.shape[1] // steps
        so = steps // 2
        ro = nwo.shape[1] // so
        in_specs += [pl.BlockSpec((None, ri, nwi.shape[2]), lambda i: (nlayer, i, 0)),
                     pl.BlockSpec((None, ro, nwo.shape[2]), lambda i: (nlayer, jnp.minimum(i, so - 1), 0))]
        args += [nwi, nwo]
        out_specs += [pl.BlockSpec((ri, nwi.shape[2]), lambda i: (i, 0)),
                      pl.BlockSpec((ro, nwo.shape[2]), lambda i: (jnp.minimum(i, so - 1), 0))]
        out_shape += [jax.ShapeDtypeStruct(nwi.shape[1:], BF16), jax.ShapeDtypeStruct(nwo.shape[1:], BF16)]
    res = pl.pallas_call(
        functools.partial(_ffn_kernel, alpha=alpha, has_mix=mix is not None, has_ple=ple is not None,
                          has_next=nxt is not None),
        grid=(steps,),
        in_specs=in_specs,
        out_specs=out_specs,
        out_shape=out_shape,
        scratch_shapes=[pltpu.VMEM((tm, D_FF), BF16)],
        compiler_params=_cparams("arbitrary"),
        name="ffn",
    )(*args)
    return res[0], tuple(res[1:])


def _proj_kernel(*refs, roped, dil):
    refs = list(refs)
    slab_ref = refs.pop() if dil > 1 else None
    if any(roped):
        x_ref, w_ref, cos_ref, sin_ref, o_ref = refs
    else:
        x_ref, w_ref, o_ref = refs
    tm = x_ref.shape[1]
    rows = tm // dil

    def by_class(ref128):
        return jnp.concatenate([ref128[pl.ds(r, rows, stride=dil), :] for r in range(dil)], axis=0)

    if dil > 1:
        n_slabs = x_ref.shape[2] // LANES
        for s in range(n_slabs):
            slab_ref[s] = x_ref[0, :, s * LANES:(s + 1) * LANES]
        x = jnp.concatenate([by_class(slab_ref.at[s]) for s in range(n_slabs)], axis=1)
    else:
        x = x_ref[0]
    if any(roped):
        cos, sin = (by_class(cos_ref), by_class(sin_ref)) if dil > 1 else (cos_ref[...], sin_ref[...])

    def store(col, val):
        for r in range(dil):
            o_ref[0, r, :, col:col + val.shape[1]] = val[r * rows:(r + 1) * rows].astype(o_ref.dtype)

    xb = x.astype(BF16)
    per_chunk = PROJ_CHUNK // QUAD_W
    for c in range(len(roped) // per_chunk):
        acc = _dot(xb, w_ref[:, c * PROJ_CHUNK:(c + 1) * PROJ_CHUNK])
        for k in range(per_chunk):
            lo = k * QUAD_W
            col = c * PROJ_CHUNK + lo
            if roped[c * per_chunk + k]:
                x1, x2 = acc[:, lo:lo + LANES], acc[:, lo + LANES:lo + QUAD_W]
                store(col, x1 * cos - x2 * sin)
                store(col + LANES, x2 * cos + x1 * sin)
            else:
                store(col, acc[:, lo:lo + QUAD_W])


def _proj(x, w, dil, roped, tables=None, tm=512):
    bsz, s, d = x.shape
    n = w.shape[1]
    in_specs = [pl.BlockSpec((1, tm, d), lambda b, i: (b, i, 0)), _const_spec(w.shape)]
    args = [x, w]
    if any(roped):
        tab_spec = pl.BlockSpec((tm, LANES), lambda b, i: (i, 0))
        in_specs += [tab_spec, tab_spec]
        args += list(tables)
    return pl.pallas_call(
        functools.partial(_proj_kernel, roped=tuple(roped), dil=dil),
        grid=(bsz, s // tm),
        in_specs=in_specs,
        out_specs=pl.BlockSpec((1, dil, tm // dil, n), lambda b, i: (b, 0, i, 0)),
        out_shape=jax.ShapeDtypeStruct((bsz, dil, s // dil, n), BF16),
        scratch_shapes=[pltpu.VMEM((d // LANES, tm, LANES), F32)] if dil > 1 else [],
        compiler_params=_cparams("parallel", "parallel"),
        name="mixer_proj",
    )(*args)


def _sb_kernel(q_ref, k_ref, v_ref, o_ref, acc_ref, r_ref, *, tq):
    i = pl.program_id(1)
    has_prev = i > 0
    d_start = pl.multiple_of(i * tq, tq)
    p_start = pl.multiple_of(jnp.maximum(i - 1, 0) * tq, tq)
    in_a = lax.broadcasted_iota(jnp.int32, (1, LANES), 1) < HEAD_DIM
    row = lax.broadcasted_iota(jnp.int32, (tq, tq), 0)
    col = lax.broadcasted_iota(jnp.int32, (tq, tq), 1)
    suffix = (row >= col).astype(BF16)
    row2 = lax.broadcasted_iota(jnp.int32, (2 * tq, tq), 0) % tq
    col2 = lax.broadcasted_iota(jnp.int32, (2 * tq, tq), 1)
    strict = col2 < row2

    def stacked_q(sl):
        q2 = q_ref[0, 0, :, sl]
        zero = jnp.zeros_like(q2)
        return jnp.concatenate([jnp.where(in_a, q2, zero), jnp.where(in_a, zero, q2)], axis=0)

    def softplus(z):
        return jnp.maximum(z, 0.0) + jnp.log2(1.0 + jnp.exp2(-jnp.abs(z)))

    def fold(pv):
        return jnp.where(in_a, pv[:tq], pv[tq:])

    for p in range(PAIRS):
        sl = slice(p * LANES, (p + 1) * LANES)
        qs = stacked_q(sl)
        z = _dot_nt(qs, k_ref[0, 0, pl.ds(d_start, tq), sl])
        sp = jnp.where(strict, softplus(z), 0.0)
        a = jnp.where(strict, jnp.exp2(z - _dot(sp.astype(BF16), suffix)), 0.0)
        pv = _dot(a.astype(BF16), v_ref[0, 0, pl.ds(d_start, tq), sl])
        r = jnp.sum(sp, axis=1, keepdims=True)
        z = _dot_nt(qs, k_ref[0, 0, pl.ds(p_start, tq), sl])
        sp = softplus(z)
        a = jnp.where(has_prev, jnp.exp2(z - _dot(sp.astype(BF16), suffix) - r), 0.0)
        pv = pv + _dot(a.astype(BF16), v_ref[0, 0, pl.ds(p_start, tq), sl])
        acc_ref[:, sl] = fold(pv)
        r = r + jnp.where(has_prev, jnp.sum(sp, axis=1, keepdims=True), 0.0)
        r_ref[p] = r
        least = r if p == 0 else jnp.minimum(least, r)
    o_ref[0] = acc_ref[...].astype(o_ref.dtype)

    def more(state):
        j, mass = state[0], state[1]
        return jnp.logical_and(j >= 0, mass < SB_DEAD_MASS)

    def sweep(p):
        sl = slice(p * LANES, (p + 1) * LANES)

        def step(state):
            j, _, acc, r = state
            start = pl.multiple_of(j * tq, tq)
            z = _dot_nt(stacked_q(sl), k_ref[0, 0, pl.ds(start, tq), sl])
            sp = softplus(z)
            a = jnp.exp2(z - _dot(sp.astype(BF16), suffix) - r)
            acc = acc + fold(_dot(a.astype(BF16), v_ref[0, 0, pl.ds(start, tq), sl]))
            r = r + jnp.sum(sp, axis=1, keepdims=True)
            return (j - 1, jnp.min(r), acc, r)

        r = r_ref[p]
        state = lax.while_loop(more, step, (i - 2, jnp.min(r), acc_ref[:, sl], r))
        o_ref[0, :, sl] = state[2].astype(o_ref.dtype)

    @pl.when(jnp.logical_and(i >= 2, jnp.min(least) < SB_DEAD_MASS))
    def _():
        for p in range(PAIRS):
            sweep(p)


def _sb_attention(h, tq=256):
    bsz, _, s, _ = h.shape
    whole = lambda blk: pl.BlockSpec((1, 1, s, D_MODEL), lambda b, i: (b, 0, 0, blk), pipeline_mode=pl.Buffered(1))
    return pl.pallas_call(
        functools.partial(_sb_kernel, tq=tq),
        grid=(bsz, s // tq),
        in_specs=[pl.BlockSpec((1, 1, tq, D_MODEL), lambda b, i: (b, 0, i, 0)), whole(1), whole(2)],
        out_specs=pl.BlockSpec((1, tq, D_MODEL), lambda b, i: (b, i, 0)),
        out_shape=jax.ShapeDtypeStruct((bsz, s, D_MODEL), BF16),
        scratch_shapes=[pltpu.VMEM((tq, D_MODEL), F32), pltpu.VMEM((PAIRS, 2 * tq, 1), F32)],
        compiler_params=_cparams("parallel", "arbitrary"),
        name="sb_attention",
    )(h, h, h)


def _band_bias(max_dist):
    row = lax.broadcasted_iota(jnp.int32, (QUAD * BLK, 2 * BLK), 0)
    kj = lax.broadcasted_iota(jnp.int32, (QUAD * BLK, 2 * BLK), 1)
    diff = BLK + row % BLK - kj
    band = (diff >= 0) & (diff <= max_dist)
    return jnp.where(jnp.stack([band & (kj >= BLK), band]), 0.0, NEG_INF).astype(F32)


def _band_lane_masks():
    lane = lax.broadcasted_iota(jnp.int32, (1, QUAD_W), 1)
    qk_sel = [(lane % LANES) // HALF_DIM == hq for hq in range(QUAD)]
    in_a = lax.broadcasted_iota(jnp.int32, (1, LANES), 1) < HEAD_DIM
    return qk_sel, in_a


def _band_quad(q4, kk, bias, qk_sel, sink=None):
    zero = jnp.zeros_like(q4)
    qs = jnp.concatenate([jnp.where(qk_sel[hq], q4, zero) for hq in range(QUAD)], axis=0)
    s = _dot_nt(qs, kk) + bias
    m = jnp.max(s, axis=1, keepdims=True)
    if sink is not None:
        m = jnp.maximum(m, sink)
    e = jnp.exp2(s - m)
    den = jnp.sum(e, axis=1, keepdims=True)
    if sink is not None:
        den = den + jnp.exp2(sink - m)
    return e.astype(BF16), den, m * LN2 + jnp.log(den)


def _band_kernel(sink_ref, bias_ref, q_ref, kp_ref, kc_ref, vp_ref, vc_ref, o_ref, *, kv_quad):
    qk_sel, in_a = _band_lane_masks()
    which = jnp.minimum(pl.program_id(1), 1)
    head_of_row = lax.broadcasted_iota(jnp.int32, (QUAD * BLK, 1), 0) // BLK
    for g in range(N_QUADS):
        c0 = kv_quad(g) * QUAD_W
        kk = jnp.concatenate([kp_ref[0, 0, :, c0:c0 + QUAD_W], kc_ref[0, 0, :, c0:c0 + QUAD_W]], axis=0)
        sink = jnp.full((QUAD * BLK, 1), sink_ref[QUAD * g] * LOG2E, F32)
        for hq in range(1, QUAD):
            sink = jnp.where(head_of_row == hq, sink_ref[QUAD * g + hq] * LOG2E, sink)
        eb, den, _ = _band_quad(q_ref[0, 0, :, g * QUAD_W:(g + 1) * QUAD_W], kk, bias_ref[which], qk_sel, sink)
        for pr in range(QUAD // 2):
            c1 = c0 + pr * LANES
            rows = slice(2 * pr * BLK, 2 * (pr + 1) * BLK)
            vv = jnp.concatenate([vp_ref[0, 0, :, c1:c1 + LANES], vc_ref[0, 0, :, c1:c1 + LANES]], axis=0)
            o2 = _dot(eb[rows], vv) / den[rows]
            col = g * QUAD_W + pr * LANES
            o_ref[0, :, col:col + LANES] = jnp.where(in_a, o2[:BLK], o2[BLK:]).astype(o_ref.dtype)


def _band_attention(h, max_dist, kv_width, kv_quad, sinks):
    bsz, _, s, _ = h.shape
    k_blk = D_MODEL // kv_width
    bias = _band_bias(max_dist)
    cur = lambda blk: (lambda b, n: (b, 0, n, blk))
    prev = lambda blk: (lambda b, n: (b, 0, jnp.maximum(n - 1, 0), blk))
    kv_spec = lambda imap: pl.BlockSpec((1, 1, BLK, kv_width), imap)
    return pl.pallas_call(
        functools.partial(_band_kernel, kv_quad=kv_quad),
        grid=(bsz, s // BLK),
        in_specs=[pl.BlockSpec(memory_space=pltpu.SMEM), _const_spec(bias.shape), pl.BlockSpec((1, 1, BLK, D_MODEL), cur(0)),
                  kv_spec(prev(k_blk)), kv_spec(cur(k_blk)), kv_spec(prev(k_blk + 1)), kv_spec(cur(k_blk + 1))],
        out_specs=pl.BlockSpec((1, BLK, D_MODEL), lambda b, n: (b, n, 0)),
        out_shape=jax.ShapeDtypeStruct((bsz, s, D_MODEL), BF16),
        compiler_params=_cparams("parallel", "arbitrary"),
        name="band_attention",
    )(sinks, bias, h, h, h, h, h)


def _band_dil_kernel(bias_ref, q_ref, kp_ref, kc_ref, vp_ref, vc_ref, o_ref, lse_ref, *, dil, quads):
    qk_sel, in_a = _band_lane_masks()
    which = jnp.minimum(pl.program_id(1), 1)

    head_lane = lax.broadcasted_iota(jnp.int32, (1, LANES), 1)
    first_head = pl.program_id(2) * (quads * QUAD)

    def one_class(r):
        dst = pl.ds(r, BLK, stride=dil) if dil > 1 else slice(None)
        lse_tile = jnp.zeros((BLK, LANES), F32)
        for g in range(quads):
            cols = slice(g * QUAD_W, (g + 1) * QUAD_W)
            kk = jnp.concatenate([kp_ref[0, r, :, cols], kc_ref[0, r, :, cols]], axis=0)
            eb, den, lse = _band_quad(q_ref[0, r, :, cols], kk, bias_ref[which], qk_sel)
            for hq in range(QUAD):
                lse_tile = jnp.where(head_lane == first_head + (g * QUAD + hq), lse[hq * BLK:(hq + 1) * BLK],
                                     lse_tile)
            for pr in range(QUAD // 2):
                c1 = slice(g * QUAD_W + pr * LANES, g * QUAD_W + (pr + 1) * LANES)
                rows = slice(2 * pr * BLK, 2 * (pr + 1) * BLK)
                vv = jnp.concatenate([vp_ref[0, r, :, c1], vc_ref[0, r, :, c1]], axis=0)
                o2 = _dot(eb[rows], vv) / den[rows]
                o_ref[0, 2 * g + pr, dst, :] = jnp.where(in_a, o2[:BLK], o2[BLK:])
        lse_ref[0, 0, dst, :] = lse_tile

    if dil == 1:
        one_class(0)
    else:
        def body(r2, carry):
            one_class(2 * r2)
            one_class(2 * r2 + 1)
            return carry
        lax.fori_loop(0, dil // 2, body, 0)


def _band_dil_attention(h, max_dist):
    bsz, dil, ls, _ = h.shape
    quads = N_QUADS if dil <= QUAD else N_QUADS // 2
    width = quads * QUAD_W
    nq = D_MODEL // width
    bias = _band_bias(max_dist)
    cur = lambda base: (lambda b, n, g: (b, 0, n, base + g))
    prev = lambda base: (lambda b, n, g: (b, 0, jnp.maximum(n - 1, 0), base + g))
    spec = lambda imap: pl.BlockSpec((1, dil, BLK, width), imap)
    o_spec = pl.BlockSpec((1, 2 * quads, BLK * dil, LANES), lambda b, n, g: (b, g, n, 0))
    o_shape = jax.ShapeDtypeStruct((bsz, PAIRS, ls * dil, LANES), F32)
    return pl.pallas_call(
        functools.partial(_band_dil_kernel, dil=dil, quads=quads),
        grid=(bsz, ls // BLK, nq),
        in_specs=[_const_spec(bias.shape), spec(cur(0)), spec(prev(nq)), spec(cur(nq)), spec(prev(2 * nq)),
                  spec(cur(2 * nq))],
        out_specs=[o_spec, pl.BlockSpec((1, 1, BLK * dil, LANES), lambda b, n, g: (b, g, n, 0))],
        out_shape=[o_shape, jax.ShapeDtypeStruct((bsz, nq, ls * dil, LANES), F32)],
        compiler_params=_cparams("parallel", "arbitrary", "arbitrary"),
        name="band_dil_attention",
    )(bias, h, h, h, h, h)


def _dil_out_ln_kernel(o0_ref, o1_ref, o2_ref, l0_ref, l1_ref, l2_ref, spread_ref, w_ref, x_ref, g_ref, b_ref,
                       y_ref, *, alpha):
    def per_head(l_ref):
        lc = l_ref[0, 0]
        for k in range(1, l_ref.shape[1]):
            lc = lc + l_ref[0, k]
        return lc

    def spread(wc):
        hi = wc.astype(BF16)
        lo = (wc - hi.astype(F32)).astype(BF16)
        return _dot(hi, spread_ref[...]) + _dot(lo, spread_ref[...])

    l0, l1, l2 = per_head(l0_ref), per_head(l1_ref), per_head(l2_ref)
    m = jnp.maximum(jnp.maximum(l0, l1), l2)
    e0, e1, e2 = jnp.exp(l0 - m), jnp.exp(l1 - m), jnp.exp(l2 - m)
    inv = 1.0 / (e0 + e1 + e2)
    w0, w1, w2 = spread(e0 * inv), spread(e1 * inv), spread(e2 * inv)
    parts = []
    for p in range(PAIRS):
        sl = slice(p * LANES, (p + 1) * LANES)
        merged = w0[:, sl] * o0_ref[0, p] + w1[:, sl] * o1_ref[0, p] + w2[:, sl] * o2_ref[0, p]
        parts.append(merged.astype(BF16))
    mix = _dot(jnp.concatenate(parts, axis=1), w_ref[...])
    y_ref[0] = _layernorm(alpha * x_ref[0] + mix, g_ref[...], b_ref[...])


def _dil_out_ln(outs, lses, w, layer, x, ln, alpha, tm=256):
    bsz, s, d = x.shape
    g, b, ln_row = ln
    slab = pl.BlockSpec((1, PAIRS, tm, LANES), lambda bi, i: (bi, 0, i, 0))
    lse_specs = [pl.BlockSpec((1, l.shape[1], tm, LANES), lambda bi, i: (bi, 0, i, 0)) for l in lses]
    spread = (np.arange(LANES)[:, None] == np.arange(d)[None, :] // HEAD_DIM).astype(np.float32)
    spread = jnp.asarray(spread, BF16)
    row = pl.BlockSpec((1, tm, d), lambda bi, i: (bi, i, 0))
    return pl.pallas_call(
        functools.partial(_dil_out_ln_kernel, alpha=alpha),
        grid=(bsz, s // tm),
        in_specs=[slab] * 3 + lse_specs + [_const_spec(spread.shape), _layer_spec(w, layer), row,
                  _layer_spec(g, ln_row), _layer_spec(b, ln_row)],
        out_specs=row,
        out_shape=jax.ShapeDtypeStruct((bsz, s, d), F32),
        compiler_params=_cparams("parallel", "parallel"),
        name="dil_out_ln",
    )(*outs, *lses, spread, w, x, g, b)


PERM_QUAD = 0
PERM_KDUP = 1
PERM_VDUP = 1 + SWA_KV_HEADS


def _selection_matrices():
    sel = np.zeros((1 + 2 * SWA_KV_HEADS, QUAD_W, QUAD_W), np.float32)
    dims = np.arange(HALF_DIM)
    for half in range(2):
        for hq in range(QUAD):
            sel[PERM_QUAD, hq * HEAD_DIM + half * HALF_DIM + dims, half * LANES + hq * HALF_DIM + dims] = 1.0
            for kv in range(SWA_KV_HEADS):
                sel[PERM_KDUP + kv, kv * HEAD_DIM + half * HALF_DIM + dims, half * LANES + hq * HALF_DIM + dims] = 1.0
    dims = np.arange(HEAD_DIM)
    for kv in range(SWA_KV_HEADS):
        for rep in range(QUAD):
            sel[PERM_VDUP + kv, SWA_KV_HEADS * HEAD_DIM + kv * HEAD_DIM + dims, rep * HEAD_DIM + dims] = 1.0
    return jnp.asarray(sel, BF16)


def _prep_kernel(w_ref, sel_ref, o_ref, *, plan):
    for j, (src, perm, scale) in enumerate(plan):
        blk = w_ref[:, src:src + QUAD_W]
        if scale != 1.0:
            blk = blk * scale
        blk = blk.astype(BF16)
        if perm is not None:
            blk = _dot(blk, sel_ref[perm]).astype(BF16)
        o_ref[:, j * QUAD_W:(j + 1) * QUAD_W] = blk


def _prep_w(w, layer, col_block, width, plan, sel, rows=256):
    d = w.shape[1]
    return pl.pallas_call(
        functools.partial(_prep_kernel, plan=tuple(plan)),
        grid=(d // rows,),
        in_specs=[pl.BlockSpec((None, rows, width), lambda i: (layer, i, col_block)), _const_spec(sel.shape)],
        out_specs=pl.BlockSpec((rows, len(plan) * QUAD_W), lambda i: (i, 0)),
        out_shape=jax.ShapeDtypeStruct((d, len(plan) * QUAD_W), BF16),
        compiler_params=_cparams("parallel"),
        name="prep_w",
    )(w, sel)


def _rope_tables(seq):
    pos = jnp.arange(seq, dtype=F32)
    inv = ROPE_THETA ** (-jnp.arange(0, HEAD_DIM, 2, dtype=F32) / HEAD_DIM)
    ang = pos[:, None] * jnp.tile(inv, LANES // HALF_DIM)[None, :]
    return jnp.cos(ang), jnp.sin(ang)


def kernel(x, p, ffn1_w_in, ffn1_w_out, ffn2_w_in, ffn2_w_out, ln_g, ln_b, sb_w_in, sb_w_out, swa_w_in,
           swa_sinks, swa_w_out, dil_w_in, dil_w_out, ple_w_proj, ple_w_gate):
    bsz, seq, d = x.shape
    depth = p.shape[0]
    t = bsz * seq
    alpha = (2 * depth) ** 0.25
    q_scale = HEAD_DIM ** -0.5 * LOG2E
    width = HEADS * HEAD_DIM
    tables = _rope_tables(seq)
    sel = _selection_matrices()
    ffn_w = (_cast_bf16(ffn1_w_in, n_layers=1)[0], _cast_bf16(ffn1_w_out, 0.5, n_layers=1)[0])
    ple_gate, ple_proj = _cast_bf16(ple_w_gate), _cast_bf16(ple_w_proj)
    sb_wo, swa_wo, dil_wo = _cast_bf16(sb_w_out), _cast_bf16(swa_w_out), _cast_bf16(dil_w_out)
    p = p.reshape(depth, t, PLE_DIM)
    n_ln = ln_g.shape[1]
    ln_g = ln_g.reshape(depth * n_ln, 1, d)
    ln_b = ln_b.reshape(depth * n_ln, 1, d)
    x = x.reshape(t, d)
    for i in range(depth):
        kind, j = i % N_MIXERS, i // N_MIXERS
        ln = lambda k: (ln_g, ln_b, i * n_ln + k)
        x, ffn_w = _ffn(x, *ffn_w, ln(0), alpha, nxt=(ffn2_w_in, ffn2_w_out, i))
        x3d = x.reshape(bsz, seq, d)
        if kind == 0:
            plan = [(QUAD_W * k, None, q_scale if k < N_QUADS else 1.0) for k in range(3 * N_QUADS)]
            w = _prep_w(sb_w_in, j, 0, 3 * width, plan, sel)
            h = _proj(x3d, w, 1, (False,) * (3 * N_QUADS))
            mix = (_sb_attention(h).reshape(t, width), sb_wo, j, ln(1))
        elif kind == 1:
            plan = ([(QUAD_W * k, PERM_QUAD, q_scale) for k in range(N_QUADS)]
                    + [(width, PERM_KDUP + kv, 1.0) for kv in range(SWA_KV_HEADS)]
                    + [(width, PERM_VDUP + kv, 1.0) for kv in range(SWA_KV_HEADS)])
            w = _prep_w(swa_w_in, j, 0, swa_w_in.shape[2], plan, sel)
            roped = (True,) * (N_QUADS + SWA_KV_HEADS) + (False,) * SWA_KV_HEADS
            h = _proj(x3d, w, 1, roped, tables)
            o = _band_attention(h, SWA_WINDOW - 1, SWA_KV_HEADS * QUAD_W, lambda g: g // (N_QUADS // SWA_KV_HEADS),
                                swa_sinks[j])
            mix = (o.reshape(t, width), swa_wo, j, ln(1))
        else:
            outs, lses = [], []
            for gi, (win, dil) in enumerate(DIL_GROUPS):
                plan = ([(QUAD_W * k, PERM_QUAD, q_scale) for k in range(N_QUADS)]
                        + [(QUAD_W * k, PERM_QUAD, 1.0) for k in range(N_QUADS, 2 * N_QUADS)]
                        + [(QUAD_W * k, None, 1.0) for k in range(2 * N_QUADS, 3 * N_QUADS)])
                w = _prep_w(dil_w_in, j, gi, 3 * width, plan, sel)
                h = _proj(x3d, w, dil, (True,) * (2 * N_QUADS) + (False,) * N_QUADS, tables)
                o, lse = _band_dil_attention(h, win // dil)
                outs.append(o)
                lses.append(lse)
            x = _dil_out_ln(outs, lses, dil_wo, j, x3d, ln(1), alpha).reshape(t, d)
            mix = None
        nxt = (ffn1_w_in, ffn1_w_out, i + 1) if i + 1 < depth else None
        x, ffn_w = _ffn(x, *ffn_w, ln(2), alpha, ple=(p, ple_gate, ple_proj, i), mix=mix, nxt=nxt)
    return x.reshape(bsz, seq, d)
```
